```python
import jax, jax.numpy as jnp
from jax import lax
import numpy as np

D_MODEL = 1024
BATCH = 4
SEQ = 4096
DEPTH = 2
DEC_BATCH = 32
DEC_SEQ = 4
PAST_LEN = 16384
PAGE_SIZE = 128

HEAD_DIM = 64
Q_BLOCK = 128
GLA_HEADS = 4
GLA_DK = 32
GLA_DV = 64
GLA_RANK = 16
GLA_TAU = 16.0
GLA_CHUNK = 64
LRU_WIDTH = 256
LRU_HEADS = 4
LRU_BLOCK = LRU_WIDTH // LRU_HEADS
CONV_W = 4
LRU_C = 8.0
SB_HEADS = 4
NSA_HEADS = 4
NSA_KV_HEADS = 1
NSA_GROUP = NSA_HEADS // NSA_KV_HEADS
NSA_BLOCK = 64
NSA_TOPK = 16
NSA_WINDOW = 512
NSA_FORCE = 1.0e4
ROPE_THETA = 500000.0
ROPE_DIM = HEAD_DIM // 4
N_EXPERTS = 16
N_GROUPS = 4
EXPERTS_PER_GROUP = N_EXPERTS // N_GROUPS
TOP_K = 2
D_EXPERT = 512
DN_ALPHA = (2.0 * DEPTH) ** 0.25
DN_BETA = (8.0 * DEPTH) ** -0.25
EPS = 1e-5
NEG = -1e30

MIX_WIDTH = GLA_HEADS * GLA_DV + LRU_WIDTH + SB_HEADS * HEAD_DIM + NSA_HEADS * HEAD_DIM
IN_SIZES = (GLA_HEADS * GLA_DK, GLA_HEADS * GLA_DK, GLA_HEADS * GLA_DV, GLA_HEADS * GLA_DV, GLA_RANK,
            LRU_WIDTH, LRU_WIDTH,
            SB_HEADS * HEAD_DIM, SB_HEADS * HEAD_DIM, SB_HEADS * HEAD_DIM,
            NSA_HEADS * HEAD_DIM) + (NSA_KV_HEADS * HEAD_DIM,) * 6 + (3 * NSA_HEADS,)
D_IN = sum(IN_SIZES)
IN_OFFSETS = tuple(int(v) for v in np.cumsum(IN_SIZES)[:-1])

kernel_name = 'hymba_gla_lru_sb_nsa_moe_step'


def layer_norm(x, g, b):
    xf = x.astype(jnp.float32)
    mu = xf.mean(-1, keepdims=True)
    var = jnp.square(xf - mu).mean(-1, keepdims=True)
    return ((xf - mu) * lax.rsqrt(var + EPS) * g + b).astype(x.dtype)


def split_heads(t, n):
    return t.reshape(t.shape[:-1] + (n, HEAD_DIM))


def partial_rope(x, pos):
    half = ROPE_DIM // 2
    inv = ROPE_THETA ** (-jnp.arange(0, ROPE_DIM, 2, dtype=jnp.float32) / ROPE_DIM)
    ang = pos.astype(jnp.float32)[:, None] * inv[None, :]
    cos = jnp.cos(ang)[None, :, None, :]
    sin = jnp.sin(ang)[None, :, None, :]
    xr = x[..., :ROPE_DIM].astype(jnp.float32)
    x1, x2 = xr[..., :half], xr[..., half:]
    rot = jnp.concatenate([x1 * cos - x2 * sin, x2 * cos + x1 * sin], -1).astype(x.dtype)
    return jnp.concatenate([rot, x[..., ROPE_DIM:]], -1)


def masked_softmax(s, mask):
    s = jnp.where(mask, s, NEG)
    e = jnp.where(mask, jnp.exp(s - s.max(-1, keepdims=True)), 0.0)
    return e / jnp.maximum(e.sum(-1, keepdims=True), 1e-30)


def gather_pages(pool, page_table):
    g = pool[page_table]
    return g.reshape((g.shape[0], -1) + pool.shape[2:])


def gla_chunk(q, k, v, log_a, s0):
    c = q.shape[1]
    b = jnp.cumsum(log_a, axis=1)
    causal = jnp.tril(jnp.ones((c, c), bool))
    diff = b[:, :, None] - b[:, None, :]
    decay = jnp.exp(jnp.where(causal[None, :, :, None, None], diff, -jnp.inf))
    att = jnp.einsum('bthk,bshk,btshk->bhts', q, k, decay)
    o = jnp.einsum('bhts,bshv->bthv', att, v) + jnp.einsum('bthk,bhkv->bthv', q * jnp.exp(b), s0)
    b_last = b[:, -1]
    s_new = jnp.exp(b_last)[..., None] * s0 + jnp.einsum('bshk,bshv->bhkv', k * jnp.exp(b_last[:, None] - b), v)
    return o, s_new


def gla_mixer(q, k, v, g, a_low, w_a2, b_a, norm_g, s0, chunk):
    bsz, L = q.shape[:2]
    f32 = jnp.float32
    q = q.astype(f32).reshape(bsz, L, GLA_HEADS, GLA_DK) * GLA_DK ** -0.5
    k = k.astype(f32).reshape(bsz, L, GLA_HEADS, GLA_DK)
    v = v.astype(f32).reshape(bsz, L, GLA_HEADS, GLA_DV)
    log_a = (jax.nn.log_sigmoid(a_low.astype(f32) @ w_a2.astype(f32) + b_a) / GLA_TAU).reshape(bsz, L, GLA_HEADS, GLA_DK)
    n_chunks = L // chunk

    def chunks(t):
        return t.reshape((bsz, n_chunks, chunk) + t.shape[2:]).swapaxes(0, 1)

    def step(s, xs):
        o, s = gla_chunk(xs[0], xs[1], xs[2], xs[3], s)
        return s, o

    s_fin, o = lax.scan(step, s0.astype(f32), (chunks(q), chunks(k), chunks(v), chunks(log_a)))
    o = o.swapaxes(0, 1).reshape(bsz, L, GLA_HEADS, GLA_DV)
    o = o * lax.rsqrt(jnp.mean(o * o, -1, keepdims=True) + EPS) * norm_g
    y = o.reshape(bsz, L, GLA_HEADS * GLA_DV) * jax.nn.silu(g.astype(f32))
    return y.astype(g.dtype), s_fin


def lru_mixer(xb, gate, conv_w, conv_b, w_a, b_a, w_x, b_x, lam, h0, buf0):
    bsz, L, _ = xb.shape
    f32 = jnp.float32
    xx = jnp.concatenate([buf0.astype(xb.dtype), xb], axis=1)
    xc = sum(xx[:, j:j + L] * conv_w[j] for j in range(CONV_W)) + conv_b
    new_buf = xx[:, L:]
    xf = xc.astype(f32)
    xh = xf.reshape(bsz, L, LRU_HEADS, LRU_BLOCK)
    r = jax.nn.sigmoid(jnp.einsum('blhi,hij->blhj', xh, w_a.astype(f32)).reshape(bsz, L, LRU_WIDTH) + b_a)
    i = jax.nn.sigmoid(jnp.einsum('blhi,hij->blhj', xh, w_x.astype(f32)).reshape(bsz, L, LRU_WIDTH) + b_x)
    log_a = -LRU_C * r * jax.nn.softplus(-lam.astype(f32))
    u = jnp.sqrt(-jnp.expm1(2.0 * log_a)) * (i * xf)

    def step(h, au):
        h = au[0] * h + au[1]
        return h, h

    h_fin, hs = lax.scan(step, h0.astype(f32), (jnp.exp(log_a).swapaxes(0, 1), u.swapaxes(0, 1)))
    y = hs.swapaxes(0, 1) * jax.nn.gelu(gate.astype(f32))
    return y.astype(xb.dtype), h_fin, new_buf


def sb_attend(q, k, v, q_pos, k_pos):
    z = jnp.einsum('bqhd,bkhd->bhqk', q, k).astype(jnp.float32) * (q.shape[-1] ** -0.5)
    mask = k_pos[None, :] < q_pos[:, None]
    log_stay = jnp.where(mask, jax.nn.log_sigmoid(-z), 0.0)
    between = lax.cumsum(log_stay, axis=3, reverse=True) - log_stay
    w = jnp.where(mask, jnp.exp(jax.nn.log_sigmoid(z) + between), 0.0)
    return jnp.einsum('bhqk,bkhd->bqhd', w.astype(v.dtype), v)


def sb_prompt(q, k, v):
    bsz, L = q.shape[:2]
    nb = L // Q_BLOCK
    pos = jnp.arange(L)
    qb = q.reshape(bsz, nb, Q_BLOCK, SB_HEADS, HEAD_DIM).swapaxes(0, 1)

    def blk(args):
        q_i, i = args
        return sb_attend(q_i, k, v, i * Q_BLOCK + jnp.arange(Q_BLOCK), pos)

    o = lax.map(blk, (qb, jnp.arange(nb)))
    return o.swapaxes(0, 1).reshape(bsz, L, SB_HEADS * HEAD_DIM)


def nsa_blocks(rows):
    blk = rows.reshape((rows.shape[0], -1, NSA_BLOCK) + rows.shape[2:])
    cmp = blk[:, :, :, :2].astype(jnp.float32).mean(2).astype(rows.dtype)
    return cmp[:, :, 0], cmp[:, :, 1], blk[:, :, :, 2], blk[:, :, :, 3]


def nsa_attend(q, q_pos, kc, vc, ks, vs, kw, vw, kw_pos, gates):
    bsz, nq = q.shape[:2]
    nb = kc.shape[1]
    scale = HEAD_DIM ** -0.5
    qg = q.reshape(bsz, nq, NSA_KV_HEADS, NSA_GROUP, HEAD_DIM)
    blk_idx = jnp.arange(nb)
    s_c = jnp.einsum('bqgrd,bngd->bgrqn', qg, kc).astype(jnp.float32) * scale
    valid_c = (blk_idx[None, :] + 1) * NSA_BLOCK - 1 <= q_pos[:, None]
    p_c = masked_softmax(s_c, valid_c)
    o_c = jnp.einsum('bgrqn,bngd->bqgrd', p_c, vc.astype(jnp.float32))
    imp = p_c.sum(axis=2)
    is_cur = blk_idx[None, :] == (q_pos // NSA_BLOCK)[:, None]
    reach = blk_idx[None, :] * NSA_BLOCK <= q_pos[:, None]
    score = jnp.where(is_cur, NSA_FORCE, jnp.where(reach, imp, -1.0))
    n_sel = min(NSA_TOPK, nb)
    _, idx = lax.top_k(score, n_sel)
    take = jax.vmap(jax.vmap(lambda kk, ii: kk[ii]))
    k_sel = take(ks.transpose(0, 3, 1, 2, 4), idx)
    v_sel = take(vs.transpose(0, 3, 1, 2, 4), idx)
    tok_pos = idx[..., None] * NSA_BLOCK + jnp.arange(NSA_BLOCK)
    valid_s = (tok_pos <= q_pos[:, None, None]).reshape(bsz, NSA_KV_HEADS, nq, n_sel * NSA_BLOCK)[:, :, None]
    s_s = jnp.einsum('bqgrd,bgqnld->bgrqnl', qg, k_sel).astype(jnp.float32) * scale
    p_s = masked_softmax(s_s.reshape(bsz, NSA_KV_HEADS, NSA_GROUP, nq, n_sel * NSA_BLOCK), valid_s)
    o_s = jnp.einsum('bgrqm,bgqmd->bqgrd', p_s,
                     v_sel.reshape(bsz, NSA_KV_HEADS, nq, n_sel * NSA_BLOCK, HEAD_DIM).astype(jnp.float32))
    s_w = jnp.einsum('bqgrd,bwgd->bgrqw', qg, kw).astype(jnp.float32) * scale
    dist = q_pos[:, None] - kw_pos[None, :]
    valid_w = (dist >= 0) & (dist <= NSA_WINDOW) & (kw_pos[None, :] >= 0)
    p_w = masked_softmax(s_w, valid_w)
    o_w = jnp.einsum('bgrqw,bwgd->bqgrd', p_w, vw.astype(jnp.float32))
    g = gates.reshape(bsz, nq, NSA_KV_HEADS, NSA_GROUP, 3)
    o = g[..., 0:1] * o_c + g[..., 1:2] * o_s + g[..., 2:3] * o_w
    return o.reshape(bsz, nq, NSA_HEADS * HEAD_DIM).astype(q.dtype)


def nsa_prompt(q, rows, win_rows, gates):
    bsz, L = q.shape[:2]
    kc, vc, ks, vs = nsa_blocks(rows)
    win_pad = jnp.pad(win_rows, ((0, 0), (NSA_WINDOW, 0), (0, 0), (0, 0), (0, 0)))
    nqb = L // Q_BLOCK
    q_b = q.reshape(bsz, nqb, Q_BLOCK, NSA_HEADS, HEAD_DIM).swapaxes(0, 1)
    g_b = gates.reshape(bsz, nqb, Q_BLOCK, NSA_HEADS, 3).swapaxes(0, 1)

    def blk(args):
        q_i, g_i, i = args
        start = i * Q_BLOCK
        w_i = lax.dynamic_slice_in_dim(win_pad, start, NSA_WINDOW + Q_BLOCK, axis=1)
        kw_pos = start - NSA_WINDOW + jnp.arange(NSA_WINDOW + Q_BLOCK)
        return nsa_attend(q_i, start + jnp.arange(Q_BLOCK), kc, vc, ks, vs,
                          w_i[:, :, 0], w_i[:, :, 1], kw_pos, g_i)

    o = lax.map(blk, (q_b, g_b, jnp.arange(nqb)))
    return o.swapaxes(0, 1).reshape(bsz, L, NSA_HEADS * HEAD_DIM)


def mixer(x, pos, p, past):
    bsz, L, _ = x.shape
    f32 = jnp.float32
    prompt = past is None
    (gq, gk, gv, gg, ga, lx, lg, sq, sk, sv, nq, nkc, nvc, nks, nvs, nkw, nvw, ngate) = jnp.split(
        x @ p['w_in'], IN_OFFSETS, axis=-1)
    s0 = jnp.zeros((bsz, GLA_HEADS, GLA_DK, GLA_DV), f32) if prompt else past['gla']
    y_a, gla_state = gla_mixer(gq, gk, gv, gg, ga, p['gla_w_a2'], p['gla_b_a'], p['gla_norm_g'], s0,
                               GLA_CHUNK if prompt else L)
    h0 = jnp.zeros((bsz, LRU_WIDTH), f32) if prompt else past['lru_h']
    buf0 = jnp.zeros((bsz, CONV_W - 1, LRU_WIDTH), x.dtype) if prompt else past['lru_conv']
    y_b, lru_h, lru_buf = lru_mixer(lx, lg, p['lru_conv_w'], p['lru_conv_b'], p['lru_w_a'], p['lru_b_a'],
                                    p['lru_w_x'], p['lru_b_x'], p['lru_lambda'], h0, buf0)
    q_c, k_c, v_c = [split_heads(t, SB_HEADS) for t in (sq, sk, sv)]
    sb_rows = jnp.stack([k_c, v_c], axis=2)
    if prompt:
        y_c = sb_prompt(q_c, k_c, v_c)
    else:
        past_kv = past['sb'].astype(x.dtype)
        k_all = jnp.concatenate([past_kv[:, :, 0], k_c], axis=1)
        v_all = jnp.concatenate([past_kv[:, :, 1], v_c], axis=1)
        y_c = sb_attend(q_c, k_all, v_all, pos, jnp.arange(k_all.shape[1])).reshape(bsz, L, SB_HEADS * HEAD_DIM)
    q_d = partial_rope(split_heads(nq, NSA_HEADS), pos)
    rows = jnp.stack([partial_rope(split_heads(nkc, NSA_KV_HEADS), pos), split_heads(nvc, NSA_KV_HEADS),
                      partial_rope(split_heads(nks, NSA_KV_HEADS), pos), split_heads(nvs, NSA_KV_HEADS)], axis=2)
    win_rows = jnp.stack([partial_rope(split_heads(nkw, NSA_KV_HEADS), pos), split_heads(nvw, NSA_KV_HEADS)], axis=2)
    gates = jax.nn.sigmoid(ngate.astype(f32)).reshape(bsz, L, NSA_HEADS, 3)
    if prompt:
        y_d = nsa_prompt(q_d, rows, win_rows, gates)
        win_state = win_rows[:, -min(NSA_WINDOW, L):]
    else:
        past_len = past['sb'].shape[1]
        full = jnp.concatenate([past['nsa'].astype(x.dtype), rows], axis=1)
        full = jnp.pad(full, ((0, 0), (0, (-full.shape[1]) % NSA_BLOCK), (0, 0), (0, 0), (0, 0)))
        kc, vc, ks, vs = nsa_blocks(full)
        n_buf = past['win'].shape[1]
        win = jnp.concatenate([past['win'].astype(x.dtype), win_rows], axis=1)
        kw_pos = past_len - n_buf + jnp.arange(win.shape[1])
        y_d = nsa_attend(q_d, pos, kc, vc, ks, vs, win[:, :, 0], win[:, :, 1], kw_pos, gates)
        win_state = win[:, -n_buf:]
    y = jnp.concatenate([y_a, y_b, y_c, y_d], axis=-1) @ p['w_out']
    return y, (sb_rows, rows, win_state, gla_state, lru_h, lru_buf)


def moe(x, router_w, w_gate, w_up, w_down):
    shp = x.shape
    xt = x.reshape(-1, shp[-1])
    probs = jax.nn.softmax((xt @ router_w).astype(jnp.float32), axis=-1)
    grp = lax.top_k(probs.reshape(-1, N_GROUPS, EXPERTS_PER_GROUP), TOP_K)[0].sum(-1)
    g_sel = jnp.argmax(grp, axis=-1)
    in_grp = (jnp.arange(N_EXPERTS) // EXPERTS_PER_GROUP)[None, :] == g_sel[:, None]
    top_w, top_i = lax.top_k(jnp.where(in_grp, probs, -1.0), TOP_K)
    top_w = top_w / top_w.sum(-1, keepdims=True)
    combine = (jax.nn.one_hot(top_i, N_EXPERTS, dtype=jnp.float32) * top_w[..., None]).sum(1)
    h = jax.nn.silu(jnp.einsum('nd,edf->nef', xt, w_gate)) * jnp.einsum('nd,edf->nef', xt, w_up)
    y = jnp.einsum('nef,efd->nd', h * combine[..., None].astype(h.dtype), w_down)
    return y.reshape(shp)


def setup_inputs(seed: int = 0) -> dict:
    key = jax.random.key(seed)
    ks = iter(jax.random.split(key, 48))
    f32 = jnp.float32

    def nrm(shape, scale=1.0):
        return jax.random.normal(next(ks), shape, f32) * scale

    n_pages = PAST_LEN // PAGE_SIZE
    n_used = DEC_BATCH * n_pages
    n_pool = n_used + max(1, n_used // 4)
    page_table = jax.random.permutation(next(ks), n_pool)[:n_used].reshape(DEC_BATCH, n_pages).astype(jnp.int32)
    win = min(NSA_WINDOW, PAST_LEN)
    a0 = jax.random.uniform(next(ks), (DEPTH, LRU_WIDTH), f32, 0.9, 0.999) ** (1.0 / LRU_C)
    lam = jnp.log(a0) - jnp.log1p(-a0)
    return {
        'x_prompt': nrm((BATCH, SEQ, D_MODEL)),
        'x_sample': nrm((DEC_BATCH, DEC_SEQ, D_MODEL)),
        'cache_sb_kv': nrm((DEPTH, n_pool, PAGE_SIZE, 2, SB_HEADS, HEAD_DIM)),
        'cache_nsa_kv': nrm((DEPTH, n_pool, PAGE_SIZE, 4, NSA_KV_HEADS, HEAD_DIM)),
        'cache_nsa_win': nrm((DEPTH, DEC_BATCH, win, 2, NSA_KV_HEADS, HEAD_DIM)),
        'state_gla': nrm((DEPTH, DEC_BATCH, GLA_HEADS, GLA_DK, GLA_DV), 0.5),
        'state_lru_h': nrm((DEPTH, DEC_BATCH, LRU_WIDTH), 0.5),
        'state_lru_conv': nrm((DEPTH, DEC_BATCH, CONV_W - 1, LRU_WIDTH)),
        'page_table': page_table,
        'ln_in_g': 1.0 + nrm((D_MODEL,), 0.02),
        'ln_in_b': nrm((D_MODEL,), 0.02),
        'w_in': nrm((DEPTH, D_MODEL, D_IN), D_MODEL ** -0.5),
        'gla_w_a2': nrm((DEPTH, GLA_RANK, GLA_HEADS * GLA_DK), GLA_RANK ** -0.5),
        'gla_b_a': nrm((DEPTH, GLA_HEADS * GLA_DK), 0.1),
        'gla_norm_g': 1.0 + nrm((DEPTH, GLA_HEADS, GLA_DV), 0.02),
        'lru_conv_w': nrm((DEPTH, CONV_W, LRU_WIDTH), CONV_W ** -0.5),
        'lru_conv_b': nrm((DEPTH, LRU_WIDTH), 0.02),
        'lru_w_a': nrm((DEPTH, LRU_HEADS, LRU_BLOCK, LRU_BLOCK), LRU_BLOCK ** -0.5),
        'lru_b_a': nrm((DEPTH, LRU_WIDTH), 0.02),
        'lru_w_x': nrm((DEPTH, LRU_HEADS, LRU_BLOCK, LRU_BLOCK), LRU_BLOCK ** -0.5),
        'lru_b_x': nrm((DEPTH, LRU_WIDTH), 0.02),
        'lru_lambda': lam,
        'w_out': nrm((DEPTH, MIX_WIDTH, D_MODEL), MIX_WIDTH ** -0.5 * DN_BETA),
        'ln1_g': 1.0 + nrm((DEPTH, D_MODEL), 0.02),
        'ln1_b': nrm((DEPTH, D_MODEL), 0.02),
        'router_w': nrm((D_MODEL, N_EXPERTS), D_MODEL ** -0.5),
        'moe_w_gate': nrm((DEPTH, N_EXPERTS, D_MODEL, D_EXPERT), D_MODEL ** -0.5),
        'moe_w_up': nrm((DEPTH, N_EXPERTS, D_MODEL, D_EXPERT), D_MODEL ** -0.5),
        'moe_w_down': nrm((DEPTH, N_EXPERTS, D_EXPERT, D_MODEL), D_EXPERT ** -0.5 * DN_BETA),
        'ln2_g': 1.0 + nrm((DEPTH, D_MODEL), 0.02),
        'ln2_b': nrm((DEPTH, D_MODEL), 0.02),
    }


def reference(x_prompt, x_sample, cache_sb_kv, cache_nsa_kv, cache_nsa_win, state_gla, state_lru_h,
              state_lru_conv, page_table, ln_in_g, ln_in_b, w_in, gla_w_a2, gla_b_a, gla_norm_g,
              lru_conv_w, lru_conv_b, lru_w_a, lru_b_a, lru_w_x, lru_b_x, lru_lambda, w_out, ln1_g, ln1_b,
              router_w, moe_w_gate, moe_w_up, moe_w_down, ln2_g, ln2_b):
    past_len = page_table.shape[1] * cache_sb_kv.shape[2]
    pos_p = jnp.arange(x_prompt.shape[1])
    pos_s = past_len + jnp.arange(x_sample.shape[1])
    xp = layer_norm(x_prompt, ln_in_g, ln_in_b)
    xs = layer_norm(x_sample, ln_in_g, ln_in_b)
    states_p, states_s = [], []
    for l in range(DEPTH):
        p = {'w_in': w_in[l], 'gla_w_a2': gla_w_a2[l], 'gla_b_a': gla_b_a[l], 'gla_norm_g': gla_norm_g[l],
             'lru_conv_w': lru_conv_w[l], 'lru_conv_b': lru_conv_b[l], 'lru_w_a': lru_w_a[l],
             'lru_b_a': lru_b_a[l], 'lru_w_x': lru_w_x[l], 'lru_b_x': lru_b_x[l],
             'lru_lambda': lru_lambda[l], 'w_out': w_out[l]}
        past = {'sb': gather_pages(cache_sb_kv[l], page_table), 'nsa': gather_pages(cache_nsa_kv[l], page_table),
                'win': cache_nsa_win[l], 'gla': state_gla[l], 'lru_h': state_lru_h[l],
                'lru_conv': state_lru_conv[l]}
        hp, st_p = mixer(xp, pos_p, p, None)
        hs, st_s = mixer(xs, pos_s, p, past)
        xp = layer_norm(DN_ALPHA * xp + hp, ln1_g[l], ln1_b[l])
        xs = layer_norm(DN_ALPHA * xs + hs, ln1_g[l], ln1_b[l])
        xp = layer_norm(DN_ALPHA * xp + moe(xp, router_w, moe_w_gate[l], moe_w_up[l], moe_w_down[l]), ln2_g[l], ln2_b[l])
        xs = layer_norm(DN_ALPHA * xs + moe(xs, router_w, moe_w_gate[l], moe_w_up[l], moe_w_down[l]), ln2_g[l], ln2_b[l])
        states_p.append(st_p)
        states_s.append(st_s)
    sb_p, nsa_p, win_p, gla_p, lruh_p, lruc_p = [jnp.stack([st[i] for st in states_p]) for i in range(6)]
    sb_s, nsa_s, win_s, gla_s, lruh_s, lruc_s = [jnp.stack([st[i] for st in states_s]) for i in range(6)]
    return (xp, xs, sb_p, sb_s, nsa_p, nsa_s, win_p, win_s, gla_p, gla_s, lruh_p, lruh_s, lruc_p, lruc_s)
```

```python
import functools

import jax
import jax.numpy as jnp
import numpy as np
from jax import lax
from jax.experimental import pallas as pl
from jax.experimental.pallas import tpu as pltpu

F32 = jnp.float32
BF16 = jnp.bfloat16

D_MODEL = 1024
HEAD_DIM = 64
GLA_HEADS, GLA_DK, GLA_DV, GLA_RANK, GLA_TAU = 4, 32, 64, 16, 16.0
LRU_WIDTH, LRU_HEADS, CONV_W, LRU_C = 256, 4, 4, 8.0
LRU_BLOCK = LRU_WIDTH // LRU_HEADS
SB_HEADS = 4
NSA_HEADS, NSA_BLOCK, NSA_TOPK, NSA_WINDOW, NSA_FORCE = 4, 64, 16, 512, 1.0e4
ROPE_THETA, ROPE_DIM = 500000.0, HEAD_DIM // 4
N_EXPERTS, N_GROUPS, TOP_K, D_EXPERT = 16, 4, 2, 512
EXPERTS_PER_GROUP = N_EXPERTS // N_GROUPS
DEPTH = 2
DN_ALPHA = (2.0 * DEPTH) ** 0.25
EPS = 1e-5
NEG = -1e30

QK_W = GLA_HEADS * GLA_DK
GV_W = GLA_HEADS * GLA_DV
HW = 4 * HEAD_DIM
IN_SIZES = (QK_W, QK_W, GV_W, GV_W, GLA_RANK, LRU_WIDTH, LRU_WIDTH, HW, HW, HW, HW) + (HEAD_DIM,) * 6 + (12,)
IN_OFFSETS = tuple(int(v) for v in np.cumsum(IN_SIZES)[:-1])
P_GLA, P_LRU, P_SBQ, P_SBKV, P_NQ, P_NSA, P_WIN, P_MISC = 0, 768, 1280, 1536, 2048, 2304, 2560, 2688
P_TOTAL = 2816
LANES = 128
ATT_BLK = 128
VMEM_LIMIT = 56 * 1024 * 1024

NN = (((1,), (0,)), ((), ()))
NT = (((1,), (1,)), ((), ()))
TN = (((0,), (0,)), ((), ()))


def _cparams(sem):
    return pltpu.CompilerParams(dimension_semantics=sem, vmem_limit_bytes=VMEM_LIMIT)


def _split2(a):
    hi = a.astype(BF16)
    lo = (a - hi.astype(F32)).astype(BF16)
    return hi, lo


def _split3(a):
    hi = a.astype(BF16)
    r = a - hi.astype(F32)
    mid = r.astype(BF16)
    lo = (r - mid.astype(F32)).astype(BF16)
    return hi, mid, lo


def _dg(a, b, dims=NN):
    return lax.dot_general(a, b, dims, preferred_element_type=F32)


def _dot1(a, b, dims=NN):
    return _dg(a.astype(BF16), b.astype(BF16), dims)


def _dot3(a, b, dims=NN):
    ah, al = _split2(a)
    bh, bl = _split2(b)
    return _dg(ah, bh, dims) + (_dg(ah, bl, dims) + _dg(al, bh, dims))


def _dot_xl(a, b_exact, dims=NN):
    h, m, l = _split3(a)
    return _dg(h, b_exact, dims) + (_dg(m, b_exact, dims) + _dg(l, b_exact, dims))


def _dot_lx(a_exact, b, dims=NN):
    h, m, l = _split3(b)
    return _dg(a_exact, h, dims) + (_dg(a_exact, m, dims) + _dg(a_exact, l, dims))


def _softplus(x):
    return jnp.maximum(x, 0.0) + jnp.log1p(jnp.exp(-jnp.abs(x)))


def _log_sigmoid(x):
    return -_softplus(-x)


def _layer_norm(x, g, b):
    mu = jnp.mean(x, axis=-1, keepdims=True)
    xc = x - mu
    var = jnp.mean(xc * xc, axis=-1, keepdims=True)
    return xc * lax.rsqrt(var + EPS) * g + b


def _iota(shape, dim):
    return lax.broadcasted_iota(jnp.int32, shape, dim)


def _ln_kernel(x_ref, g_ref, b_ref, o_ref):
    o_ref[...] = _layer_norm(x_ref[...], g_ref[...], b_ref[...])


def _ln_call(x, g, b, tm):
    n, d = x.shape
    return pl.pallas_call(
        _ln_kernel,
        grid=(n // tm,),
        in_specs=[pl.BlockSpec((tm, d), lambda i: (i, 0)),
                  pl.BlockSpec((1, d), lambda i: (0, 0)),
                  pl.BlockSpec((1, d), lambda i: (0, 0))],
        out_specs=pl.BlockSpec((tm, d), lambda i: (i, 0)),
        out_shape=jax.ShapeDtypeStruct((n, d), F32),
        compiler_params=_cparams(("parallel",)),
        name="ln_in",
    )(x, g.reshape(1, d), b.reshape(1, d))


def _rope(v, c, s, flags):
    ones = jnp.ones_like(c)
    zeros = jnp.zeros_like(s)
    cc = jnp.concatenate([c if f else ones for f in flags], axis=-1)
    ss = jnp.concatenate([s if f else zeros for f in flags], axis=-1)
    n = v.shape[-1]
    half = ROPE_DIM // 2
    lane = _iota(v.shape, 1) % HEAD_DIM
    sw = jnp.where(lane < half, pltpu.roll(v, n - half, 1), pltpu.roll(v, half, 1))
    return v * cc + sw * ss


def _proj_kernel(x_ref, w_ref, tab_ref, gla_ref, lru_ref, sbq_ref, sbkv_ref, nq_ref, nsa_ref, win_ref, misc_ref):
    x = x_ref[...].astype(BF16)

    def mm(off, width):
        return jnp.dot(x, w_ref[:, off:off + width], preferred_element_type=F32)

    gla_ref[...] = mm(P_GLA, 768)
    lru_ref[...] = mm(P_LRU, 512)
    sbq_ref[...] = mm(P_SBQ, 256)
    sbkv_ref[...] = mm(P_SBKV, 512)
    tab = tab_ref[...]
    c = tab[:, :HEAD_DIM]
    s = tab[:, HEAD_DIM:]
    nq_ref[...] = _rope(mm(P_NQ, 256), c, s, (1, 1, 1, 1))
    nsa_ref[...] = _rope(mm(P_NSA, 256), c, s, (1, 0, 1, 0))
    win_ref[...] = _rope(mm(P_WIN, 128), c, s, (1, 0))
    misc_ref[...] = mm(P_MISC, 128)


def _proj_call(x, w, tab, tm):
    n = x.shape[0]
    tab_blocks = tab.shape[0] // tm
    widths = (768, 512, 256, 512, 256, 256, 128, 128)
    return pl.pallas_call(
        _proj_kernel,
        grid=(n // tm,),
        in_specs=[pl.BlockSpec((tm, D_MODEL), lambda i: (i, 0)),
                  pl.BlockSpec((D_MODEL, P_TOTAL), lambda i: (0, 0)),
                  pl.BlockSpec((tm, LANES), lambda i: (i % tab_blocks, 0))],
        out_specs=[pl.BlockSpec((tm, wd), lambda i: (i, 0)) for wd in widths],
        out_shape=[jax.ShapeDtypeStruct((n, wd), F32) for wd in widths],
        compiler_params=_cparams(("parallel",)),
        name="in_proj",
    )(x, w, tab)


def _rope_table(pos):
    half = ROPE_DIM // 2
    inv = ROPE_THETA ** (-jnp.arange(0, ROPE_DIM, 2, dtype=F32) / ROPE_DIM)
    ang = pos.astype(F32)[:, None] * inv[None, :]
    cos, sin = jnp.cos(ang), jnp.sin(ang)
    n = pos.shape[0]
    c = jnp.concatenate([cos, cos, jnp.ones((n, HEAD_DIM - ROPE_DIM), F32)], axis=-1)
    s = jnp.concatenate([-sin, sin, jnp.zeros((n, HEAD_DIM - ROPE_DIM), F32)], axis=-1)
    del half
    return jnp.concatenate([c, s], axis=-1)


def _prep_w_in(w):
    parts = jnp.split(w, IN_OFFSETS, axis=-1)
    (gq, gk, gv, gg, ga, lx, lg, sq, sk, sv, nq, nkc, nvc, nks, nvs, nkw, nvw, ngate) = parts
    pad = jnp.zeros((w.shape[0], LANES - GLA_RANK - 12), w.dtype)
    return jnp.concatenate([gq, gk, gv, gg, lx, lg, sq, sk, sv, nq, nkc, nvc, nks, nvs, nkw, nvw,
                            ga, ngate, pad], axis=-1).astype(BF16)


def _gla_kernel(q_ref, k_ref, v_ref, g_ref, misc_ref, wa2_ref, ba_ref, ng_ref, indv_ref, bd_ref, ones_ref,
                s0_ref, y_ref, st_ref, st_scr, b_scr, phi_scr, plo_scr, att_scr, *, C, Cv):
    c = pl.program_id(1)

    @pl.when(c == 0)
    def _():
        st_scr[...] = s0_ref[...]

    q = q_ref[...] * (GLA_DK ** -0.5)
    k = k_ref[...]
    v = v_ref[...]
    ga = misc_ref[:, :GLA_RANK]
    la = _log_sigmoid(_dot3(ga, wa2_ref[...]) + ba_ref[...]) * (1.0 / GLA_TAU)
    rows = _iota((C, QK_W), 0)
    if Cv < C:
        la = jnp.where(rows < Cv, la, 0.0)
    tril = (_iota((C, C), 0) >= _iota((C, C), 1)).astype(BF16)
    b = _dot_lx(tril, la)
    b_scr[...] = b

    def fill(s, carry):
        b_s = b_scr[pl.ds(s, 1), :]
        k_s = k_ref[pl.ds(s, 1), :]
        d = jnp.where(rows >= s, b - b_s, -jnp.inf)
        p = q * k_s * jnp.exp(d)
        hi, lo = _split2(p)
        off = pl.multiple_of(s * C, C)
        phi_scr[pl.ds(off, C), :] = hi
        plo_scr[pl.ds(off, C), :] = lo
        return carry

    lax.fori_loop(0, Cv, fill, 0)
    indv = indv_ref[...]
    att_scr[:Cv * C, :] = _dg(phi_scr[:Cv * C, :], indv) + _dg(plo_scr[:Cv * C, :], indv)

    def gather(s, o):
        off = pl.multiple_of(s * C, C)
        return o + att_scr[pl.ds(off, C), :] * v_ref[pl.ds(s, 1), :]

    o = lax.fori_loop(0, Cv, gather, jnp.zeros((C, GV_W), F32))
    st = st_scr[...]
    o = o + _dot3(q * jnp.exp(b), st, NT)
    bl = b[C - 1:C, :]
    kd = k * jnp.exp(bl - b)
    upd = _dot3(v, kd, TN) * bd_ref[...]
    st_new = st * jnp.exp(bl) + upd
    st_scr[...] = st_new
    st_ref[...] = st_new
    ms = _dot_xl(o * o, ones_ref[...])
    on = o * lax.rsqrt(ms + EPS) * ng_ref[...]
    g = g_ref[...]
    y_ref[...] = on * (g * jax.nn.sigmoid(g))


def _gla_call(gla, misc, wa2, ba, ng, s0_t, C, Cv):
    bsz, L, _ = gla.shape
    nc = L // C
    indv = (np.arange(QK_W)[:, None] // GLA_DK == np.arange(GV_W)[None, :] // GLA_DV)
    bd = (np.arange(GV_W)[:, None] // GLA_DV == np.arange(QK_W)[None, :] // GLA_DK)
    ones = (np.arange(GV_W)[:, None] // GLA_DV == np.arange(GV_W)[None, :] // GLA_DV) / float(GLA_DV)
    const = lambda shape: pl.BlockSpec(shape, lambda b, c: (0,) * len(shape))
    kern = functools.partial(_gla_kernel, C=C, Cv=Cv)
    return pl.pallas_call(
        kern,
        grid=(bsz, nc),
        in_specs=[pl.BlockSpec((None, C, QK_W), lambda b, c: (b, c, 0)),
                  pl.BlockSpec((None, C, QK_W), lambda b, c: (b, c, 1)),
                  pl.BlockSpec((None, C, GV_W), lambda b, c: (b, c, 1)),
                  pl.BlockSpec((None, C, GV_W), lambda b, c: (b, c, 2)),
                  pl.BlockSpec((None, C, LANES), lambda b, c: (b, c, 0)),
                  const((GLA_RANK, QK_W)), const((1, QK_W)), const((1, GV_W)),
                  const((QK_W, GV_W)), const((GV_W, QK_W)), const((GV_W, GV_W)),
                  pl.BlockSpec((None, GV_W, QK_W), lambda b, c: (b, 0, 0))],
        out_specs=[pl.BlockSpec((None, C, GV_W), lambda b, c: (b, c, 0)),
                   pl.BlockSpec((None, GV_W, QK_W), lambda b, c: (b, 0, 0))],
        out_shape=[jax.ShapeDtypeStruct((bsz, L, GV_W), F32),
                   jax.ShapeDtypeStruct((bsz, GV_W, QK_W), F32)],
        scratch_shapes=[pltpu.VMEM((GV_W, QK_W), F32), pltpu.VMEM((C, QK_W), F32),
                        pltpu.VMEM((C * C, QK_W), BF16), pltpu.VMEM((C * C, QK_W), BF16),
                        pltpu.VMEM((C * C, GV_W), F32)],
        compiler_params=_cparams(("parallel", "arbitrary")),
        name="gla",
    )(gla, gla, gla, gla, misc, wa2, ba.reshape(1, QK_W), ng.reshape(1, GV_W),
      jnp.asarray(indv, BF16), jnp.asarray(bd, F32), jnp.asarray(ones, BF16), s0_t)


def _gla_state_to_t(s0):
    bsz = s0.shape[0]
    st = jnp.swapaxes(s0, 2, 3)
    eye = jnp.eye(GLA_HEADS, dtype=s0.dtype)
    full = st[:, :, :, None, :] * eye[None, :, None, :, None]
    return full.reshape(bsz, GV_W, QK_W)


def _gla_state_from_t(st):
    bsz = st.shape[0]
    full = st.reshape(bsz, GLA_HEADS, GLA_DV, GLA_HEADS, GLA_DK)
    diag = jnp.stack([full[:, h, :, h, :] for h in range(GLA_HEADS)], axis=1)
    return jnp.swapaxes(diag, 2, 3)


def _lru_gates(xc, wa_ref, ba_ref, wx_ref, bx_ref, sp_ref):
    r = jax.nn.sigmoid(_dot3(xc, wa_ref[...]) + ba_ref[...])
    i = jax.nn.sigmoid(_dot3(xc, wx_ref[...]) + bx_ref[...])
    log_a = -LRU_C * r * sp_ref[...]
    a = jnp.exp(log_a)
    u = jnp.sqrt(-jnp.tanh(log_a) * (a * a + 1.0)) * (i * xc)
    return a, u


def _lru_prompt_kernel(x_ref, gate_ref, cw_ref, cb_ref, wa_ref, ba_ref, wx_ref, bx_ref, sp_ref,
                       y_ref, hfin_ref, h_scr, tail_scr, a_scr, u_scr, hs_scr, *, T):
    c = pl.program_id(1)

    @pl.when(c == 0)
    def _():
        h_scr[...] = jnp.zeros_like(h_scr)
        tail_scr[...] = jnp.zeros_like(tail_scr)

    x = x_ref[...]
    xx = jnp.concatenate([tail_scr[...], x], axis=0)
    base = 8 - (CONV_W - 1)
    xc = cb_ref[...] + sum(xx[base + j:base + j + T] * cw_ref[j:j + 1, :] for j in range(CONV_W))
    tail_scr[...] = x[T - 8:T]
    a, u = _lru_gates(xc, wa_ref, ba_ref, wx_ref, bx_ref, sp_ref)
    a_scr[...] = a
    u_scr[...] = u

    def group(gi, h):
        off = pl.multiple_of(gi * 8, 8)
        ag = a_scr[pl.ds(off, 8), :]
        ug = u_scr[pl.ds(off, 8), :]
        outs = []
        for j in range(8):
            h = ag[j:j + 1] * h + ug[j:j + 1]
            outs.append(h)
        hs_scr[pl.ds(off, 8), :] = jnp.concatenate(outs, axis=0)
        return h

    h = lax.fori_loop(0, T // 8, group, h_scr[...])
    h_scr[...] = h
    hfin_ref[...] = h
    y_ref[...] = hs_scr[...] * jax.nn.gelu(gate_ref[...])


def _lru_weights(conv_w, conv_b, w_a, b_a, w_x, b_x, lam):
    def bd(w):
        eye = jnp.eye(LRU_HEADS, dtype=w.dtype)
        return (w[:, :, None, :] * eye[:, None, :, None]).reshape(LRU_WIDTH, LRU_WIDTH)
    sp = jax.nn.softplus(-lam.astype(F32)).reshape(1, LRU_WIDTH)
    r1 = lambda t: t.reshape(1, LRU_WIDTH)
    return conv_w, r1(conv_b), bd(w_a), r1(b_a), bd(w_x), r1(b_x), sp


def _lru_prompt_call(lru, weights, T):
    bsz, L, _ = lru.shape
    W = LRU_WIDTH
    const = lambda shape: pl.BlockSpec(shape, lambda b, c: (0,) * len(shape))
    kern = functools.partial(_lru_prompt_kernel, T=T)
    return pl.pallas_call(
        kern,
        grid=(bsz, L // T),
        in_specs=[pl.BlockSpec((None, T, W), lambda b, c: (b, c, 0)),
                  pl.BlockSpec((None, T, W), lambda b, c: (b, c, 1)),
                  const((CONV_W, W)), const((1, W)), const((W, W)), const((1, W)),
                  const((W, W)), const((1, W)), const((1, W))],
        out_specs=[pl.BlockSpec((None, T, W), lambda b, c: (b, c, 0)),
                   pl.BlockSpec((None, 1, W), lambda b, c: (b, 0, 0))],
        out_shape=[jax.ShapeDtypeStruct((bsz, L, W), F32), jax.ShapeDtypeStruct((bsz, 1, W), F32)],
        scratch_shapes=[pltpu.VMEM((1, W), F32), pltpu.VMEM((8, W), F32), pltpu.VMEM((T, W), F32),
                        pltpu.VMEM((T, W), F32), pltpu.VMEM((T, W), F32)],
        compiler_params=_cparams(("parallel", "arbitrary")),
        name="lru_prompt",
    )(lru, lru, *weights)


def _lru_sample_kernel(x_ref, gate_ref, buf_ref, h0_ref, cw_ref, cb_ref, wa_ref, ba_ref, wx_ref, bx_ref, sp_ref,
                       y_ref, hfin_ref, *, L):
    xx = [buf_ref[j] for j in range(CONV_W - 1)] + [x_ref[t] for t in range(L)]
    h = h0_ref[...]
    for t in range(L):
        xc = cb_ref[...] + sum(xx[t + j] * cw_ref[j:j + 1, :] for j in range(CONV_W))
        a, u = _lru_gates(xc, wa_ref, ba_ref, wx_ref, bx_ref, sp_ref)
        h = a * h + u
        y_ref[t] = h * jax.nn.gelu(gate_ref[t])
    hfin_ref[...] = h


def _lru_sample_call(x_t, gate_t, buf_t, h0, weights):
    L, bsz, W = x_t.shape
    kern = functools.partial(_lru_sample_kernel, L=L)
    return pl.pallas_call(
        kern,
        out_shape=[jax.ShapeDtypeStruct((L, bsz, W), F32), jax.ShapeDtypeStruct((bsz, W), F32)],
        compiler_params=pltpu.CompilerParams(vmem_limit_bytes=VMEM_LIMIT),
        name="lru_sample",
    )(x_t, gate_t, buf_t, h0, *weights)


def _sb_block(qb, kb, vb, u, mask, acc, cs):
    z = _dg(qb, kb, NT)
    sp = _softplus(z)
    ls = -sp if mask is None else jnp.where(mask, -sp, 0.0)
    btw = _dot_xl(ls, u)
    e = jnp.exp(z - sp + (cs + btw))
    w = e if mask is None else jnp.where(mask, e, 0.0)
    acc = acc + _dg(w.astype(BF16), vb)
    cs = cs + jnp.sum(ls, axis=1, keepdims=True)
    return acc, cs


def _sb_prompt_kernel(q_ref, kv_ref, u_ref, o_ref):
    i = pl.program_id(1)
    row = _iota((ATT_BLK, ATT_BLK), 0)
    col = _iota((ATT_BLK, ATT_BLK), 1)
    u = u_ref[...]
    scale = HEAD_DIM ** -0.5
    outs = []
    for h in range(SB_HEADS):
        qh = (q_ref[:, h * HEAD_DIM:(h + 1) * HEAD_DIM] * scale).astype(BF16)

        def body(jj, carry, h=h, qh=qh):
            acc, cs = carry
            j = i - jj
            off = pl.multiple_of(j * ATT_BLK, ATT_BLK)
            kb = kv_ref[pl.ds(off, ATT_BLK), h * HEAD_DIM:(h + 1) * HEAD_DIM].astype(BF16)
            vb = kv_ref[pl.ds(off, ATT_BLK), HW + h * HEAD_DIM:HW + (h + 1) * HEAD_DIM].astype(BF16)
            mask = (j * ATT_BLK + col) < (i * ATT_BLK + row)
            return _sb_block(qh, kb, vb, u, mask, acc, cs)

        acc, _ = lax.fori_loop(0, i + 1, body,
                               (jnp.zeros((ATT_BLK, HEAD_DIM), F32), jnp.zeros((ATT_BLK, 1), F32)))
        outs.append(acc)
    o_ref[...] = jnp.concatenate(outs, axis=1)


def _later_keys_matrix():
    return jnp.asarray(np.arange(ATT_BLK)[:, None] > np.arange(ATT_BLK)[None, :], BF16)


def _sb_prompt_call(sbq, sbkv):
    bsz, L, _ = sbq.shape
    return pl.pallas_call(
        _sb_prompt_kernel,
        grid=(bsz, L // ATT_BLK),
        in_specs=[pl.BlockSpec((None, ATT_BLK, HW), lambda b, i: (b, i, 0)),
                  pl.BlockSpec((None, L, 2 * HW), lambda b, i: (b, 0, 0)),
                  pl.BlockSpec((ATT_BLK, ATT_BLK), lambda b, i: (0, 0))],
        out_specs=pl.BlockSpec((None, ATT_BLK, HW), lambda b, i: (b, i, 0)),
        out_shape=jax.ShapeDtypeStruct((bsz, L, HW), F32),
        compiler_params=_cparams(("parallel", "arbitrary")),
        name="sb_prompt",
    )(sbq, sbkv, _later_keys_matrix())


def _sb_sample_kernel(pt_ref, q_ref, new_ref, *rest, L, pps):
    page_refs = rest[:pps]
    u_ref, o_ref, acc_scr, cs_scr = rest[pps:]
    s = pl.program_id(1)
    R = SB_HEADS * L
    u = u_ref[...]
    q = q_ref[...]
    headmask = (_iota((R, HW), 0) // L) == (_iota((R, HW), 1) // HEAD_DIM)
    q16 = jnp.where(headmask, jnp.concatenate([q] * SB_HEADS, axis=0) * (HEAD_DIM ** -0.5), 0.0).astype(BF16)

    def step(ref, mask, acc, cs):
        kb = ref[:, :HW].astype(BF16)
        vb = ref[:, HW:].astype(BF16)
        z = _dg(q16, kb, NT)
        sp = _softplus(z)
        ls = -sp if mask is None else jnp.where(mask, -sp, 0.0)
        btw = _dot_xl(ls, u)
        e = jnp.exp(z - sp + (cs + btw))
        w = e if mask is None else jnp.where(mask, e, 0.0)
        r = _dg(w.astype(BF16), vb)
        acc = acc + jnp.where(headmask, r, 0.0)
        cs = cs + jnp.sum(ls, axis=1, keepdims=True)
        return acc, cs

    @pl.when(s == 0)
    def _():
        mask = _iota((R, ATT_BLK), 1) < (_iota((R, ATT_BLK), 0) % L)
        acc, cs = step(new_ref, mask, jnp.zeros((R, HW), F32), jnp.zeros((R, 1), F32))
        acc_scr[...] = acc
        cs_scr[...] = cs

    acc = acc_scr[...]
    cs = cs_scr[...]
    for r in range(pps):
        acc, cs = step(page_refs[r], None, acc, cs)
    acc_scr[...] = acc
    cs_scr[...] = cs

    @pl.when(s == pl.num_programs(1) - 1)
    def _():
        o_ref[...] = sum(acc[h * L:(h + 1) * L] for h in range(SB_HEADS))


def _sb_sample_call(sbq, new_kv, cache, page_table, layer, pps):
    bsz, L, _ = sbq.shape
    n_pages = page_table.shape[1]
    steps = n_pages // pps
    kern = functools.partial(_sb_sample_kernel, L=L, pps=pps)

    def page_spec(r):
        return pl.BlockSpec((None, None, ATT_BLK, 2 * HW),
                            lambda b, s, pt: (layer, pt[b, n_pages - 1 - (s * pps + r)], 0, 0))

    grid_spec = pltpu.PrefetchScalarGridSpec(
        num_scalar_prefetch=1,
        grid=(bsz, steps),
        in_specs=[pl.BlockSpec((None, L, HW), lambda b, s, pt: (b, 0, 0)),
                  pl.BlockSpec((None, ATT_BLK, 2 * HW), lambda b, s, pt: (b, 0, 0))]
                 + [page_spec(r) for r in range(pps)]
                 + [pl.BlockSpec((ATT_BLK, ATT_BLK), lambda b, s, pt: (0, 0))],
        out_specs=pl.BlockSpec((None, L, HW), lambda b, s, pt: (b, 0, 0)),
        scratch_shapes=[pltpu.VMEM((SB_HEADS * L, HW), F32), pltpu.VMEM((SB_HEADS * L, 1), F32)],
    )
    return pl.pallas_call(
        kern,
        grid_spec=grid_spec,
        out_shape=jax.ShapeDtypeStruct((bsz, L, HW), F32),
        compiler_params=_cparams(("parallel", "arbitrary")),
        name="sb_sample",
    )(page_table, sbq, new_kv, *([cache] * pps), _later_keys_matrix())


def _online_softmax_step(s, valid, vb, m, l, acc):
    sm = jnp.where(valid, s, NEG)
    m_new = jnp.maximum(m, jnp.max(sm, axis=1, keepdims=True))
    alpha = jnp.exp(m - m_new)
    p = jnp.where(valid, jnp.exp(s - m_new), 0.0)
    l = alpha * l + jnp.sum(p, axis=1, keepdims=True)
    acc = alpha * acc + _dg(p.astype(BF16), vb)
    return m_new, l, acc


def _masked_softmax(s, valid):
    sm = jnp.where(valid, s, NEG)
    e = jnp.where(valid, jnp.exp(sm - jnp.max(sm, axis=1, keepdims=True)), 0.0)
    return e / jnp.maximum(jnp.sum(e, axis=1, keepdims=True), 1e-30)


def _top_select(score, n_sel, floor):
    lane = _iota(score.shape, 1)
    big = score.shape[1]
    sel = jnp.zeros(score.shape, F32)
    for _ in range(n_sel):
        m = jnp.max(score, axis=1, keepdims=True)
        idx = jnp.min(jnp.where(score == m, lane, big), axis=1, keepdims=True)
        hit = lane == idx
        sel = jnp.where(hit, 1.0, sel)
        score = jnp.where(hit, floor, score)
    return sel


def _nsa_prompt_kernel(q_ref, rows_ref, win_ref, misc_ref, o_ref, kcvc_scr, *, L):
    i = pl.program_id(1)
    nb = L // NSA_BLOCK
    Q = ATT_BLK
    G = NSA_HEADS
    scale = HEAD_DIM ** -0.5

    @pl.when(i == 0)
    def _():
        blk = rows_ref[:, :2 * HEAD_DIM].reshape(nb, NSA_BLOCK, 2 * HEAD_DIM)
        kcvc_scr[...] = jnp.sum(blk, axis=1) * (1.0 / NSA_BLOCK)

    q = q_ref[...]
    q4 = (jnp.concatenate([q[:, h * HEAD_DIM:(h + 1) * HEAD_DIM] for h in range(G)], axis=0) * scale).astype(BF16)
    qpos1 = i * Q + _iota((Q, 1), 0)
    qpos4 = jnp.concatenate([qpos1] * G, axis=0)

    kc = kcvc_scr[:, :HEAD_DIM]
    vc = kcvc_scr[:, HEAD_DIM:]
    s_c = _dg(q4, kc.astype(BF16), NT)
    blk4 = _iota((G * Q, nb), 1)
    valid_c = (blk4 + 1) * NSA_BLOCK - 1 <= qpos4
    p_c = _masked_softmax(s_c, valid_c)
    o_c = _dot1(p_c, vc)

    imp = sum(p_c[h * Q:(h + 1) * Q] for h in range(G))
    blk1 = _iota((Q, nb), 1)
    is_cur = blk1 == qpos1 // NSA_BLOCK
    reach = blk1 * NSA_BLOCK <= qpos1
    score = jnp.where(is_cur, NSA_FORCE, jnp.where(reach, imp, -1.0))
    sel = _top_select(score, min(NSA_TOPK, nb), -2.0).astype(BF16)

    per_tile = ATT_BLK // NSA_BLOCK
    col = _iota((Q, ATT_BLK), 1)

    def sel_body(j, carry):
        m, l, acc = carry
        off = pl.multiple_of(j * ATT_BLK, ATT_BLK)
        kb = rows_ref[pl.ds(off, ATT_BLK), 2 * HEAD_DIM:3 * HEAD_DIM].astype(BF16)
        vb = rows_ref[pl.ds(off, ATT_BLK), 3 * HEAD_DIM:4 * HEAD_DIM].astype(BF16)
        expand = (_iota((nb, ATT_BLK), 0) == j * per_tile + _iota((nb, ATT_BLK), 1) // NSA_BLOCK).astype(BF16)
        picked = _dg(sel, expand) > 0.5
        valid1 = picked & (j * ATT_BLK + col <= qpos1)
        valid = jnp.concatenate([valid1] * G, axis=0)
        s = _dg(q4, kb, NT)
        return _online_softmax_step(s, valid, vb, m, l, acc)

    init = (jnp.full((G * Q, 1), NEG, F32), jnp.zeros((G * Q, 1), F32), jnp.zeros((G * Q, HEAD_DIM), F32))
    _, l_s, acc_s = lax.fori_loop(0, i + 1, sel_body, init)
    o_s = acc_s / jnp.maximum(l_s, 1e-30)

    n_win_tiles = NSA_WINDOW // ATT_BLK + 1
    carry = init
    for t in range(n_win_tiles):
        j = i - (n_win_tiles - 1) + t
        jc = jnp.maximum(j, 0)
        off = pl.multiple_of(jc * ATT_BLK, ATT_BLK)
        kb = win_ref[pl.ds(off, ATT_BLK), :HEAD_DIM].astype(BF16)
        vb = win_ref[pl.ds(off, ATT_BLK), HEAD_DIM:].astype(BF16)
        kpos = j * ATT_BLK + col
        dist = qpos1 - kpos
        valid1 = (dist >= 0) & (dist <= NSA_WINDOW) & (kpos >= 0)
        valid = jnp.concatenate([valid1] * G, axis=0)
        s = _dg(q4, kb, NT)
        carry = _online_softmax_step(s, valid, vb, *carry)
    _, l_w, acc_w = carry
    o_w = acc_w / jnp.maximum(l_w, 1e-30)

    gates = jax.nn.sigmoid(misc_ref[:, GLA_RANK:GLA_RANK + 3 * G])
    outs = []
    for h in range(G):
        sl = slice(h * Q, (h + 1) * Q)
        outs.append(gates[:, 3 * h:3 * h + 1] * o_c[sl] + gates[:, 3 * h + 1:3 * h + 2] * o_s[sl]
                    + gates[:, 3 * h + 2:3 * h + 3] * o_w[sl])
    o_ref[...] = jnp.concatenate(outs, axis=1)


def _nsa_prompt_call(nq, rows, win, misc):
    bsz, L, _ = nq.shape
    kern = functools.partial(_nsa_prompt_kernel, L=L)
    return pl.pallas_call(
        kern,
        grid=(bsz, L // ATT_BLK),
        in_specs=[pl.BlockSpec((None, ATT_BLK, HW), lambda b, i: (b, i, 0)),
                  pl.BlockSpec((None, L, HW), lambda b, i: (b, 0, 0)),
                  pl.BlockSpec((None, L, 2 * HEAD_DIM), lambda b, i: (b, 0, 0)),
                  pl.BlockSpec((None, ATT_BLK, LANES), lambda b, i: (b, i, 0))],
        out_specs=pl.BlockSpec((None, ATT_BLK, HW), lambda b, i: (b, i, 0)),
        out_shape=jax.ShapeDtypeStruct((bsz, L, HW), F32),
        scratch_shapes=[pltpu.VMEM((L // NSA_BLOCK, 2 * HEAD_DIM), F32)],
        compiler_params=_cparams(("parallel", "arbitrary")),
        name="nsa_prompt",
    )(nq, rows, win, misc)


def _nsa_sample_kernel(pt_ref, q_ref, new_ref, winp_ref, winn_ref, misc_ref, *rest, L, pps, n_pages):
    page_refs = rest[:pps]
    o_ref, kcvc_scr, sel_scr, oc_scr, ow_scr, m_scr, l_scr, acc_scr = rest[pps:]
    s = pl.program_id(1)
    steps = n_pages // pps
    G = NSA_HEADS
    R = G * L
    nbp = n_pages * (ATT_BLK // NSA_BLOCK)
    per_page = ATT_BLK // NSA_BLOCK
    scale = HEAD_DIM ** -0.5
    q = q_ref[...]
    q4 = (jnp.concatenate([q[:, h * HEAD_DIM:(h + 1) * HEAD_DIM] for h in range(G)], axis=0) * scale).astype(BF16)
    t_row = _iota((R, 1), 0) % L

    @pl.when(s < steps)
    def _():
        for r in range(pps):
            p = s * pps + r
            page = page_refs[r][...]
            means = [jnp.sum(page[b * NSA_BLOCK:(b + 1) * NSA_BLOCK], axis=0, keepdims=True) * (1.0 / NSA_BLOCK)
                     for b in range(per_page)]
            kcvc_scr[pl.ds(p * per_page, per_page), :] = jnp.concatenate(means, axis=0)

    @pl.when(s == steps - 1)
    def _():
        kc = kcvc_scr[:, :HEAD_DIM]
        vc = kcvc_scr[:, HEAD_DIM:]
        s_c = _dg(q4, kc.astype(BF16), NT)
        e = jnp.exp(s_c - jnp.max(s_c, axis=1, keepdims=True))
        p_c = e / jnp.maximum(jnp.sum(e, axis=1, keepdims=True), 1e-30)
        oc_scr[...] = _dot1(p_c, vc)
        imp = sum(p_c[h * L:(h + 1) * L] for h in range(G))
        sel = _top_select(imp, min(NSA_TOPK, nbp + 1) - 1, -2.0)
        sel_scr[...] = jnp.concatenate([sel] * G, axis=0)
        n_buf = winp_ref.shape[0]
        colp = _iota((R, n_buf), 1)
        s_p = _dg(q4, winp_ref[:, :HEAD_DIM].astype(BF16), NT)
        dist_p = t_row + n_buf - colp
        valid_p = (dist_p >= 0) & (dist_p <= NSA_WINDOW)
        coln = _iota((R, ATT_BLK), 1)
        s_n = _dg(q4, winn_ref[:, :HEAD_DIM].astype(BF16), NT)
        dist_n = t_row - coln
        valid_n = (dist_n >= 0) & (dist_n <= NSA_WINDOW)
        init = (jnp.full((R, 1), NEG, F32), jnp.zeros((R, 1), F32), jnp.zeros((R, HEAD_DIM), F32))
        c1 = _online_softmax_step(s_p, valid_p, winp_ref[:, HEAD_DIM:].astype(BF16), *init)
        _, l_w, acc_w = _online_softmax_step(s_n, valid_n, winn_ref[:, HEAD_DIM:].astype(BF16), *c1)
        ow_scr[...] = acc_w / jnp.maximum(l_w, 1e-30)
        s_new = _dg(q4, new_ref[:, 2 * HEAD_DIM:3 * HEAD_DIM].astype(BF16), NT)
        m0, l0, a0 = _online_softmax_step(s_new, dist_n >= 0, new_ref[:, 3 * HEAD_DIM:].astype(BF16), *init)
        m_scr[...] = m0
        l_scr[...] = l0
        acc_scr[...] = a0

    @pl.when(s >= steps)
    def _():
        sel = sel_scr[...].astype(BF16)
        carry = (m_scr[...], l_scr[...], acc_scr[...])
        for r in range(pps):
            p = (s - steps) * pps + r
            page = page_refs[r]
            expand = (_iota((nbp, ATT_BLK), 0) == p * per_page + _iota((nbp, ATT_BLK), 1) // NSA_BLOCK).astype(BF16)
            valid = _dg(sel, expand) > 0.5
            sc = _dg(q4, page[:, :HEAD_DIM].astype(BF16), NT)
            carry = _online_softmax_step(sc, valid, page[:, HEAD_DIM:].astype(BF16), *carry)
        m_scr[...], l_scr[...], acc_scr[...] = carry

    @pl.when(s == 2 * steps - 1)
    def _():
        o_s = acc_scr[...] / jnp.maximum(l_scr[...], 1e-30)
        gates = jax.nn.sigmoid(misc_ref[:, GLA_RANK:GLA_RANK + 3 * G])
        outs = []
        for h in range(G):
            sl = slice(h * L, (h + 1) * L)
            outs.append(gates[:, 3 * h:3 * h + 1] * oc_scr[sl, :] + gates[:, 3 * h + 1:3 * h + 2] * o_s[sl]
                        + gates[:, 3 * h + 2:3 * h + 3] * ow_scr[sl, :])
        o_ref[...] = jnp.concatenate(outs, axis=1)


def _nsa_sample_call(nq, new_rows, win_past, win_new, misc, cache, page_table, layer, pps):
    bsz, L, _ = nq.shape
    n_pages = page_table.shape[1]
    steps = n_pages // pps
    n_buf = win_past.shape[2]
    nbp = n_pages * (ATT_BLK // NSA_BLOCK)
    R = NSA_HEADS * L
    kern = functools.partial(_nsa_sample_kernel, L=L, pps=pps, n_pages=n_pages)

    def page_spec(r):
        return pl.BlockSpec((None, None, ATT_BLK, 2 * HEAD_DIM),
                            lambda b, s, pt: (layer, pt[b, (s % steps) * pps + r], 0, s // steps))

    grid_spec = pltpu.PrefetchScalarGridSpec(
        num_scalar_prefetch=1,
        grid=(bsz, 2 * steps),
        in_specs=[pl.BlockSpec((None, L, HW), lambda b, s, pt: (b, 0, 0)),
                  pl.BlockSpec((None, ATT_BLK, HW), lambda b, s, pt: (b, 0, 0)),
                  pl.BlockSpec((None, None, n_buf, 2 * HEAD_DIM), lambda b, s, pt: (layer, b, 0, 0)),
                  pl.BlockSpec((None, ATT_BLK, 2 * HEAD_DIM), lambda b, s, pt: (b, 0, 0)),
                  pl.BlockSpec((None, L, LANES), lambda b, s, pt: (b, 0, 0))]
                 + [page_spec(r) for r in range(pps)],
        out_specs=pl.BlockSpec((None, L, HW), lambda b, s, pt: (b, 0, 0)),
        scratch_shapes=[pltpu.VMEM((nbp, 2 * HEAD_DIM), F32), pltpu.VMEM((R, nbp), F32),
                        pltpu.VMEM((R, HEAD_DIM), F32), pltpu.VMEM((R, HEAD_DIM), F32),
                        pltpu.VMEM((R, 1), F32), pltpu.VMEM((R, 1), F32), pltpu.VMEM((R, HEAD_DIM), F32)],
    )
    return pl.pallas_call(
        kern,
        grid_spec=grid_spec,
        out_shape=jax.ShapeDtypeStruct((bsz, L, HW), F32),
        compiler_params=_cparams(("parallel", "arbitrary")),
        name="nsa_sample",
    )(page_table, nq, new_rows, win_past, win_new, misc, *([cache] * pps))


def _outproj_kernel(ya_ref, yb_ref, yc_ref, yd_ref, x_ref, w_ref, g_ref, b_ref, o_ref):
    y = jnp.concatenate([ya_ref[...], yb_ref[...], yc_ref[...], yd_ref[...]], axis=1).astype(BF16)
    h = jnp.dot(y, w_ref[...], preferred_element_type=F32)
    o_ref[...] = _layer_norm(DN_ALPHA * x_ref[...] + h, g_ref[...], b_ref[...])


def _outproj_call(ya, yb, yc, yd, x, w, g, b, tm):
    n = x.shape[0]
    part = lambda: pl.BlockSpec((tm, HW), lambda i: (i, 0))
    return pl.pallas_call(
        _outproj_kernel,
        grid=(n // tm,),
        in_specs=[part(), part(), part(), part(),
                  pl.BlockSpec((tm, D_MODEL), lambda i: (i, 0)),
                  pl.BlockSpec((4 * HW, D_MODEL), lambda i: (0, 0)),
                  pl.BlockSpec((1, D_MODEL), lambda i: (0, 0)),
                  pl.BlockSpec((1, D_MODEL), lambda i: (0, 0))],
        out_specs=pl.BlockSpec((tm, D_MODEL), lambda i: (i, 0)),
        out_shape=jax.ShapeDtypeStruct((n, D_MODEL), F32),
        compiler_params=_cparams(("parallel",)),
        name="out_proj_ln",
    )(ya, yb, yc, yd, x, w, g.reshape(1, D_MODEL), b.reshape(1, D_MODEL))


def _route(logits):
    lane = _iota(logits.shape, 1)
    live = lane < N_EXPERTS
    lg = jnp.where(live, logits, NEG)
    e = jnp.where(live, jnp.exp(lg - jnp.max(lg, axis=1, keepdims=True)), 0.0)
    probs = e / jnp.sum(e, axis=1, keepdims=True)
    big = logits.shape[1]

    def top2(vals):
        w1 = jnp.max(vals, axis=1, keepdims=True)
        i1 = jnp.min(jnp.where(vals == w1, lane, big), axis=1, keepdims=True)
        rest = jnp.where(lane == i1, -2.0, vals)
        w2 = jnp.max(rest, axis=1, keepdims=True)
        i2 = jnp.min(jnp.where(rest == w2, lane, big), axis=1, keepdims=True)
        return w1, i1, w2, i2

    best = None
    g_sel = None
    for g in range(N_GROUPS):
        in_g = (lane // EXPERTS_PER_GROUP) == g
        w1, _, w2, _ = top2(jnp.where(in_g, probs, -1.0))
        tot = w1 + w2
        if best is None:
            best, g_sel = tot, jnp.zeros_like(tot, dtype=jnp.int32)
        else:
            upd = tot > best
            g_sel = jnp.where(upd, g, g_sel)
            best = jnp.where(upd, tot, best)
    in_grp = live & ((lane // EXPERTS_PER_GROUP) == g_sel)
    w1, i1, w2, i2 = top2(jnp.where(in_grp, probs, -1.0))
    den = w1 + w2
    return jnp.where(lane == i1, w1 / den, 0.0) + jnp.where(lane == i2, w2 / den, 0.0)


def _moe_kernel(x_ref, rw_ref, wg_ref, wu_ref, wd_ref, g_ref, b_ref, o_ref, acc_scr, comb_scr, xb_scr):
    e = pl.program_id(1)

    @pl.when(e == 0)
    def _():
        x = x_ref[...]
        xb_scr[...] = x.astype(BF16)
        comb_scr[...] = _route(_dot3(x, rw_ref[...]))
        acc_scr[...] = jnp.zeros_like(acc_scr)

    xb = xb_scr[...]
    hg = jnp.dot(xb, wg_ref[...], preferred_element_type=F32)
    hu = jnp.dot(xb, wu_ref[...], preferred_element_type=F32)
    comb = comb_scr[...]
    c = jnp.sum(jnp.where(_iota(comb.shape, 1) == e, comb, 0.0), axis=1, keepdims=True)
    h = (hg * jax.nn.sigmoid(hg)) * hu * c
    acc_scr[...] += jnp.dot(h.astype(BF16), wd_ref[...], preferred_element_type=F32)

    @pl.when(e == pl.num_programs(1) - 1)
    def _():
        o_ref[...] = _layer_norm(DN_ALPHA * x_ref[...] + acc_scr[...], g_ref[...], b_ref[...])


def _moe_call(x, rw, wg, wu, wd, g, b, tm):
    n = x.shape[0]
    return pl.pallas_call(
        _moe_kernel,
        grid=(n // tm, N_EXPERTS),
        in_specs=[pl.BlockSpec((tm, D_MODEL), lambda i, e: (i, 0)),
                  pl.BlockSpec((D_MODEL, LANES), lambda i, e: (0, 0)),
                  pl.BlockSpec((None, D_MODEL, D_EXPERT), lambda i, e: (e, 0, 0)),
                  pl.BlockSpec((None, D_MODEL, D_EXPERT), lambda i, e: (e, 0, 0)),
                  pl.BlockSpec((None, D_EXPERT, D_MODEL), lambda i, e: (e, 0, 0)),
                  pl.BlockSpec((1, D_MODEL), lambda i, e: (0, 0)),
                  pl.BlockSpec((1, D_MODEL), lambda i, e: (0, 0))],
        out_specs=pl.BlockSpec((tm, D_MODEL), lambda i, e: (i, 0)),
        out_shape=jax.ShapeDtypeStruct((n, D_MODEL), F32),
        scratch_shapes=[pltpu.VMEM((tm, D_MODEL), F32), pltpu.VMEM((tm, LANES), F32),
                        pltpu.VMEM((tm, D_MODEL), BF16)],
        compiler_params=_cparams(("parallel", "arbitrary")),
        name="moe_ln",
    )(x, rw, wg, wu, wd, g.reshape(1, D_MODEL), b.reshape(1, D_MODEL))


def _pad_rows(t, n):
    return jnp.pad(t, ((0, 0), (0, n - t.shape[1]), (0, 0)))


def _row_tile(n, pref):
    t = min(pref, n)
    while n % t:
        t //= 2
    return t


def kernel(x_prompt, x_sample, cache_sb_kv, cache_nsa_kv, cache_nsa_win, state_gla, state_lru_h, state_lru_conv, page_table, ln_in_g, ln_in_b, w_in, gla_w_a2, gla_b_a, gla_norm_g, lru_conv_w, lru_conv_b, lru_w_a, lru_b_a, lru_w_x, lru_b_x, lru_lambda, w_out, ln1_g, ln1_b, router_w, moe_w_gate, moe_w_up, moe_w_down, ln2_g, ln2_b):
    bp, lp, d = x_prompt.shape
    bs, ls, _ = x_sample.shape
    depth = w_in.shape[0]
    n_pool, page = cache_sb_kv.shape[1], cache_sb_kv.shape[2]
    n_pages = page_table.shape[1]
    past_len = n_pages * page
    n_p, n_s = bp * lp, bs * ls
    assert page == ATT_BLK and lp % ATT_BLK == 0 and ls <= 8 and d == D_MODEL
    assert n_pages * (ATT_BLK // NSA_BLOCK) >= NSA_TOPK - 1 and cache_nsa_win.shape[2] <= past_len

    tm_p = _row_tile(n_p, 512)
    tm_s = _row_tile(n_s, 512)
    tab_p = _rope_table(jnp.arange(lp))
    tab_s = jnp.tile(_rope_table(past_len + jnp.arange(ls)), (bs, 1))
    assert lp % tm_p == 0 and tm_s == n_s

    cache_sb = cache_sb_kv.reshape(depth, n_pool, page, 2 * HW)
    cache_nsa = cache_nsa_kv.reshape(depth, n_pool, page, HW)
    win_past = cache_nsa_win.reshape(depth, bs, cache_nsa_win.shape[2], 2 * HEAD_DIM)
    rw = jnp.pad(router_w, ((0, 0), (0, LANES - N_EXPERTS)))
    pps = 16
    while n_pages % pps:
        pps //= 2
    gla_c = 64 if lp % 64 == 0 else ATT_BLK
    lru_t = _row_tile(lp, 512)

    xp = _ln_call(x_prompt.reshape(n_p, d), ln_in_g, ln_in_b, tm_p)
    xs = _ln_call(x_sample.reshape(n_s, d), ln_in_g, ln_in_b, tm_s)

    st_p, st_s = [], []
    for l in range(depth):
        w_l = _prep_w_in(w_in[l])
        lru_w = _lru_weights(lru_conv_w[l], lru_conv_b[l], lru_w_a[l], lru_b_a[l], lru_w_x[l], lru_b_x[l],
                             lru_lambda[l])
        w_out_l = w_out[l].astype(BF16)
        wg, wu, wd = moe_w_gate[l].astype(BF16), moe_w_up[l].astype(BF16), moe_w_down[l].astype(BF16)

        gla, lru, sbq, sbkv, nq, nsa, win, misc = _proj_call(xp, w_l, tab_p, tm_p)
        r3 = lambda t, b_, l_: t.reshape(b_, l_, t.shape[-1])
        gla3, lru3, sbq3, sbkv3 = r3(gla, bp, lp), r3(lru, bp, lp), r3(sbq, bp, lp), r3(sbkv, bp, lp)
        nq3, nsa3, win3, misc3 = r3(nq, bp, lp), r3(nsa, bp, lp), r3(win, bp, lp), r3(misc, bp, lp)
        y_a, gla_t = _gla_call(gla3, misc3, gla_w_a2[l], gla_b_a[l], gla_norm_g[l],
                               jnp.zeros((bp, GV_W, QK_W), F32), gla_c, gla_c)
        y_b, lru_h = _lru_prompt_call(lru3, lru_w, lru_t)
        y_c = _sb_prompt_call(sbq3, sbkv3)
        y_d = _nsa_prompt_call(nq3, nsa3, win3, misc3)
        x1 = _outproj_call(y_a.reshape(n_p, HW), y_b.reshape(n_p, HW), y_c.reshape(n_p, HW), y_d.reshape(n_p, HW),
                           xp, w_out_l, ln1_g[l], ln1_b[l], tm_p)
        xp = _moe_call(x1, rw, wg, wu, wd, ln2_g[l], ln2_b[l], _row_tile(n_p, 1024))
        wn = min(NSA_WINDOW, lp)
        st_p.append((sbkv3.reshape(bp, lp, 2, SB_HEADS, HEAD_DIM), nsa3.reshape(bp, lp, 4, 1, HEAD_DIM),
                     win3[:, lp - wn:].reshape(bp, wn, 2, 1, HEAD_DIM), _gla_state_from_t(gla_t),
                     lru_h.reshape(bp, LRU_WIDTH), lru3[:, lp - (CONV_W - 1):, :LRU_WIDTH]))

        gla, lru, sbq, sbkv, nq, nsa, win, misc = _proj_call(xs, w_l, tab_s, tm_s)
        gla3, lru3, sbq3, sbkv3 = r3(gla, bs, ls), r3(lru, bs, ls), r3(sbq, bs, ls), r3(sbkv, bs, ls)
        nq3, nsa3, win3, misc3 = r3(nq, bs, ls), r3(nsa, bs, ls), r3(win, bs, ls), r3(misc, bs, ls)
        gc = 16
        y_a, gla_t = _gla_call(_pad_rows(gla3, gc), _pad_rows(misc3, gc), gla_w_a2[l], gla_b_a[l], gla_norm_g[l],
                               _gla_state_to_t(state_gla[l].astype(F32)), gc, ls)
        y_a = y_a[:, :ls]
        tmaj = lambda t: jnp.swapaxes(t, 0, 1)
        y_b, lru_h = _lru_sample_call(tmaj(lru3[:, :, :LRU_WIDTH]), tmaj(lru3[:, :, LRU_WIDTH:]),
                                      tmaj(state_lru_conv[l]), state_lru_h[l].astype(F32), lru_w)
        y_b = tmaj(y_b)
        y_c = _sb_sample_call(sbq3, _pad_rows(sbkv3, ATT_BLK), cache_sb, page_table, l, pps)
        y_d = _nsa_sample_call(nq3, _pad_rows(nsa3, ATT_BLK), win_past, _pad_rows(win3, ATT_BLK), misc3,
                               cache_nsa, page_table, l, pps)
        x1 = _outproj_call(y_a.reshape(n_s, HW), y_b.reshape(n_s, HW), y_c.reshape(n_s, HW), y_d.reshape(n_s, HW),
                           xs, w_out_l, ln1_g[l], ln1_b[l], tm_s)
        xs = _moe_call(x1, rw, wg, wu, wd, ln2_g[l], ln2_b[l], tm_s)
        n_buf = cache_nsa_win.shape[2]
        win_all = jnp.concatenate([win_past[l], win3], axis=1)
        lru_buf = jnp.concatenate([state_lru_conv[l], lru3[:, :, :LRU_WIDTH]], axis=1)[:, ls:]
        st_s.append((sbkv3.reshape(bs, ls, 2, SB_HEADS, HEAD_DIM), nsa3.reshape(bs, ls, 4, 1, HEAD_DIM),
                     win_all[:, win_all.shape[1] - n_buf:].reshape(bs, n_buf, 2, 1, HEAD_DIM),
                     _gla_state_from_t(gla_t), lru_h, lru_buf))

    outs_p = [jnp.stack([st[i] for st in st_p]) for i in range(6)]
    outs_s = [jnp.stack([st[i] for st in st_s]) for i in range(6)]
    res = [xp.reshape(bp, lp, d), xs.reshape(bs, ls, d)]
    for a, b in zip(outs_p, outs_s):
        res += [a, b]
    return tuple(res)
```

```python
import functools

import jax
import jax.numpy as jnp
import numpy as np
from jax import lax
from jax.experimental import pallas as pl
from jax.experimental.pallas import tpu as pltpu

F32 = jnp.float32
BF16 = jnp.bfloat16

D_MODEL = 1024
HEAD_DIM = 64
GLA_HEADS, GLA_DK, GLA_DV, GLA_RANK, GLA_TAU = 4, 32, 64, 16, 16.0
LRU_WIDTH, LRU_HEADS, CONV_W, LRU_C = 256, 4, 4, 8.0
LRU_BLOCK = LRU_WIDTH // LRU_HEADS
SB_HEADS = 4
NSA_HEADS, NSA_BLOCK, NSA_TOPK, NSA_WINDOW, NSA_FORCE = 4, 64, 16, 512, 1.0e4
ROPE_THETA, ROPE_DIM = 500000.0, HEAD_DIM // 4
N_EXPERTS, N_GROUPS, TOP_K, D_EXPERT = 16, 4, 2, 512
EXPERTS_PER_GROUP = N_EXPERTS // N_GROUPS
DEPTH = 2
DN_ALPHA = (2.0 * DEPTH) ** 0.25
EPS = 1e-5
NEG = -1e30

QK_W = GLA_HEADS * GLA_DK
GV_W = GLA_HEADS * GLA_DV
HW = 4 * HEAD_DIM
IN_SIZES = (QK_W, QK_W, GV_W, GV_W, GLA_RANK, LRU_WIDTH, LRU_WIDTH, HW, HW, HW, HW) + (HEAD_DIM,) * 6 + (12,)
IN_OFFSETS = tuple(int(v) for v in np.cumsum(IN_SIZES)[:-1])
P_GLA, P_LRU, P_SBQ, P_SBKV, P_NQ, P_NSA, P_WIN, P_MISC = 0, 768, 1280, 1536, 2048, 2304, 2560, 2688
P_TOTAL = 2816
LANES = 128
ATT_BLK = 128
SEL_TILE = 2 * ATT_BLK
VMEM_LIMIT = 56 * 1024 * 1024

NN = (((1,), (0,)), ((), ()))
NT = (((1,), (1,)), ((), ()))
TN = (((0,), (0,)), ((), ()))


def _cparams(sem):
    return pltpu.CompilerParams(dimension_semantics=sem, vmem_limit_bytes=VMEM_LIMIT)


def _split2(a):
    hi = a.astype(BF16)
    lo = (a - hi.astype(F32)).astype(BF16)
    return hi, lo


def _split3(a):
    hi = a.astype(BF16)
    r = a - hi.astype(F32)
    mid = r.astype(BF16)
    lo = (r - mid.astype(F32)).astype(BF16)
    return hi, mid, lo


def _dg(a, b, dims=NN):
    return lax.dot_general(a, b, dims, preferred_element_type=F32)


def _dot1(a, b, dims=NN):
    return _dg(a.astype(BF16), b.astype(BF16), dims)


def _dot3(a, b, dims=NN):
    ah, al = _split2(a)
    bh, bl = _split2(b)
    return _dg(ah, bh, dims) + (_dg(ah, bl, dims) + _dg(al, bh, dims))


def _dot_xl(a, b_exact, dims=NN):
    h, m, l = _split3(a)
    return _dg(h, b_exact, dims) + (_dg(m, b_exact, dims) + _dg(l, b_exact, dims))


def _dot_lx(a_exact, b, dims=NN):
    h, m, l = _split3(b)
    return _dg(a_exact, h, dims) + (_dg(a_exact, m, dims) + _dg(a_exact, l, dims))


def _softplus(x):
    return jnp.maximum(x, 0.0) + jnp.log1p(jnp.exp(-jnp.abs(x)))


def _log_sigmoid(x):
    return -_softplus(-x)


def _layer_norm(x, g, b):
    mu = jnp.mean(x, axis=-1, keepdims=True)
    xc = x - mu
    var = jnp.mean(xc * xc, axis=-1, keepdims=True)
    return xc * lax.rsqrt(var + EPS) * g + b


def _iota(shape, dim):
    return lax.broadcasted_iota(jnp.int32, shape, dim)


def _low_half(x, fill=0.0):
    return jnp.where(_iota(x.shape, 1) < HEAD_DIM, x, fill)


def _swap_halves(x):
    return pltpu.roll(x, HEAD_DIM, 1)


def _heads_to_rows(q):
    pieces = []
    for h in range(HW // HEAD_DIM):
        pair = q[:, (h // 2) * LANES:(h // 2 + 1) * LANES]
        pieces.append(_low_half(pair if h % 2 == 0 else _swap_halves(pair)))
    return jnp.concatenate(pieces, axis=0)


def _rows_to_heads(parts):
    pairs = [jnp.where(_iota(parts[0].shape, 1) < HEAD_DIM, parts[2 * p], _swap_halves(parts[2 * p + 1]))
             for p in range(len(parts) // 2)]
    return jnp.concatenate(pairs, axis=1)


def _ln_kernel(x_ref, g_ref, b_ref, o_ref):
    o_ref[...] = _layer_norm(x_ref[...], g_ref[...], b_ref[...])


def _ln_call(x, g, b, tm):
    n, d = x.shape
    return pl.pallas_call(
        _ln_kernel,
        grid=(n // tm,),
        in_specs=[pl.BlockSpec((tm, d), lambda i: (i, 0)),
                  pl.BlockSpec((1, d), lambda i: (0, 0)),
                  pl.BlockSpec((1, d), lambda i: (0, 0))],
        out_specs=pl.BlockSpec((tm, d), lambda i: (i, 0)),
        out_shape=jax.ShapeDtypeStruct((n, d), F32),
        compiler_params=_cparams(("parallel",)),
        name="ln_in",
    )(x, g.reshape(1, d), b.reshape(1, d))


def _rope(v, c, s, flags):
    ones = jnp.ones_like(c)
    zeros = jnp.zeros_like(s)
    cc = jnp.concatenate([c if f else ones for f in flags], axis=-1)
    ss = jnp.concatenate([s if f else zeros for f in flags], axis=-1)
    n = v.shape[-1]
    half = ROPE_DIM // 2
    lane = _iota(v.shape, 1) % HEAD_DIM
    sw = jnp.where(lane < half, pltpu.roll(v, n - half, 1), pltpu.roll(v, half, 1))
    return v * cc + sw * ss


def _proj_kernel(x_ref, w_ref, tab_ref, gla_ref, lru_ref, sbq_ref, sbkv_ref, nq_ref, nsa_ref, win_ref, misc_ref):
    x = x_ref[...].astype(BF16)

    def mm(off, width):
        return jnp.dot(x, w_ref[:, off:off + width], preferred_element_type=F32)

    gla_ref[...] = mm(P_GLA, 768)
    lru_ref[...] = mm(P_LRU, 512)
    sbq_ref[...] = mm(P_SBQ, 256)
    sbkv_ref[...] = mm(P_SBKV, 512)
    tab = tab_ref[...]
    c = tab[:, :HEAD_DIM]
    s = tab[:, HEAD_DIM:]
    nq_ref[...] = _rope(mm(P_NQ, 256), c, s, (1, 1, 1, 1))
    nsa_ref[...] = _rope(mm(P_NSA, 256), c, s, (1, 0, 1, 0))
    win_ref[...] = _rope(mm(P_WIN, 128), c, s, (1, 0))
    misc_ref[...] = mm(P_MISC, 128)


def _proj_call(x, w, tab, tm):
    n = x.shape[0]
    tab_blocks = tab.shape[0] // tm
    widths = (768, 512, 256, 512, 256, 256, 128, 128)
    return pl.pallas_call(
        _proj_kernel,
        grid=(n // tm,),
        in_specs=[pl.BlockSpec((tm, D_MODEL), lambda i: (i, 0)),
                  pl.BlockSpec((D_MODEL, P_TOTAL), lambda i: (0, 0)),
                  pl.BlockSpec((tm, LANES), lambda i: (i % tab_blocks, 0))],
        out_specs=[pl.BlockSpec((tm, wd), lambda i: (i, 0)) for wd in widths],
        out_shape=[jax.ShapeDtypeStruct((n, wd), F32) for wd in widths],
        compiler_params=_cparams(("parallel",)),
        name="in_proj",
    )(x, w, tab)


def _rope_table(pos):
    inv = ROPE_THETA ** (-jnp.arange(0, ROPE_DIM, 2, dtype=F32) / ROPE_DIM)
    ang = pos.astype(F32)[:, None] * inv[None, :]
    cos, sin = jnp.cos(ang), jnp.sin(ang)
    n = pos.shape[0]
    c = jnp.concatenate([cos, cos, jnp.ones((n, HEAD_DIM - ROPE_DIM), F32)], axis=-1)
    s = jnp.concatenate([-sin, sin, jnp.zeros((n, HEAD_DIM - ROPE_DIM), F32)], axis=-1)
    return jnp.concatenate([c, s], axis=-1)


def _prep_w_in(w):
    parts = jnp.split(w, IN_OFFSETS, axis=-1)
    (gq, gk, gv, gg, ga, lx, lg, sq, sk, sv, nq, nkc, nvc, nks, nvs, nkw, nvw, ngate) = parts
    pad = jnp.zeros((w.shape[0], LANES - GLA_RANK - 12), w.dtype)
    return jnp.concatenate([gq, gk, gv, gg, lx, lg, sq, sk, sv, nq, nkc, nvc, nks, nvs, nkw, nvw,
                            ga, ngate, pad], axis=-1).astype(BF16)


def _gla_kernel(q_ref, k_ref, v_ref, g_ref, misc_ref, wa2_ref, ba_ref, ng_ref, indv_ref, bd_ref, ones_ref,
                s0_ref, y_ref, st_ref, st_scr, b_scr, phi_scr, plo_scr, att_scr, *, C, Cv):
    c = pl.program_id(1)

    @pl.when(c == 0)
    def _():
        st_scr[...] = s0_ref[...]

    q = q_ref[...] * (GLA_DK ** -0.5)
    k = k_ref[...]
    v = v_ref[...]
    ga = misc_ref[:, :GLA_RANK]
    la = _log_sigmoid(_dot3(ga, wa2_ref[...]) + ba_ref[...]) * (1.0 / GLA_TAU)
    rows = _iota((C, QK_W), 0)
    if Cv < C:
        la = jnp.where(rows < Cv, la, 0.0)
    tril = (_iota((C, C), 0) >= _iota((C, C), 1)).astype(BF16)
    b = _dot_lx(tril, la)
    b_scr[...] = b

    def fill(s, carry):
        b_s = b_scr[pl.ds(s, 1), :]
        k_s = k_ref[pl.ds(s, 1), :]
        d = jnp.where(rows >= s, b - b_s, -jnp.inf)
        p = q * k_s * jnp.exp(d)
        hi, lo = _split2(p)
        off = pl.multiple_of(s * C, C)
        phi_scr[pl.ds(off, C), :] = hi
        plo_scr[pl.ds(off, C), :] = lo
        return carry

    lax.fori_loop(0, Cv, fill, 0, unroll=4)
    indv = indv_ref[...]
    att_scr[:Cv * C, :] = _dg(phi_scr[:Cv * C, :], indv) + _dg(plo_scr[:Cv * C, :], indv)

    def gather(s, o):
        off = pl.multiple_of(s * C, C)
        return o + att_scr[pl.ds(off, C), :] * v_ref[pl.ds(s, 1), :]

    o = lax.fori_loop(0, Cv, gather, jnp.zeros((C, GV_W), F32), unroll=4)
    st = st_scr[...]
    o = o + _dot3(q * jnp.exp(b), st, NT)
    bl = b[C - 1:C, :]
    kd = k * jnp.exp(bl - b)
    upd = _dot3(v, kd, TN) * bd_ref[...]
    st_new = st * jnp.exp(bl) + upd
    st_scr[...] = st_new
    st_ref[...] = st_new
    ms = _dot_xl(o * o, ones_ref[...])
    on = o * lax.rsqrt(ms + EPS) * ng_ref[...]
    g = g_ref[...]
    y_ref[...] = on * (g * jax.nn.sigmoid(g))


def _gla_call(gla, misc, wa2, ba, ng, s0_t, C, Cv):
    bsz, L, _ = gla.shape
    nc = L // C
    indv = (np.arange(QK_W)[:, None] // GLA_DK == np.arange(GV_W)[None, :] // GLA_DV)
    bd = (np.arange(GV_W)[:, None] // GLA_DV == np.arange(QK_W)[None, :] // GLA_DK)
    ones = (np.arange(GV_W)[:, None] // GLA_DV == np.arange(GV_W)[None, :] // GLA_DV) / float(GLA_DV)
    const = lambda shape: pl.BlockSpec(shape, lambda b, c: (0,) * len(shape))
    kern = functools.partial(_gla_kernel, C=C, Cv=Cv)
    return pl.pallas_call(
        kern,
        grid=(bsz, nc),
        in_specs=[pl.BlockSpec((None, C, QK_W), lambda b, c: (b, c, 0)),
                  pl.BlockSpec((None, C, QK_W), lambda b, c: (b, c, 1)),
                  pl.BlockSpec((None, C, GV_W), lambda b, c: (b, c, 1)),
                  pl.BlockSpec((None, C, GV_W), lambda b, c: (b, c, 2)),
                  pl.BlockSpec((None, C, LANES), lambda b, c: (b, c, 0)),
                  const((GLA_RANK, QK_W)), const((1, QK_W)), const((1, GV_W)),
                  const((QK_W, GV_W)), const((GV_W, QK_W)), const((GV_W, GV_W)),
                  pl.BlockSpec((None, GV_W, QK_W), lambda b, c: (b, 0, 0))],
        out_specs=[pl.BlockSpec((None, C, GV_W), lambda b, c: (b, c, 0)),
                   pl.BlockSpec((None, GV_W, QK_W), lambda b, c: (b, 0, 0))],
        out_shape=[jax.ShapeDtypeStruct((bsz, L, GV_W), F32),
                   jax.ShapeDtypeStruct((bsz, GV_W, QK_W), F32)],
        scratch_shapes=[pltpu.VMEM((GV_W, QK_W), F32), pltpu.VMEM((C, QK_W), F32),
                        pltpu.VMEM((C * C, QK_W), BF16), pltpu.VMEM((C * C, QK_W), BF16),
                        pltpu.VMEM((C * C, GV_W), F32)],
        compiler_params=_cparams(("parallel", "arbitrary")),
        name="gla",
    )(gla, gla, gla, gla, misc, wa2, ba.reshape(1, QK_W), ng.reshape(1, GV_W),
      jnp.asarray(indv, BF16), jnp.asarray(bd, F32), jnp.asarray(ones, BF16), s0_t)


def _gla_state_to_t(s0):
    bsz = s0.shape[0]
    st = jnp.swapaxes(s0, 2, 3)
    eye = jnp.eye(GLA_HEADS, dtype=s0.dtype)
    full = st[:, :, :, None, :] * eye[None, :, None, :, None]
    return full.reshape(bsz, GV_W, QK_W)


def _gla_state_from_t(st):
    bsz = st.shape[0]
    full = st.reshape(bsz, GLA_HEADS, GLA_DV, GLA_HEADS, GLA_DK)
    diag = jnp.stack([full[:, h, :, h, :] for h in range(GLA_HEADS)], axis=1)
    return jnp.swapaxes(diag, 2, 3)


def _lru_gates(xc, wa_ref, ba_ref, wx_ref, bx_ref, sp_ref):
    r = jax.nn.sigmoid(_dot3(xc, wa_ref[...]) + ba_ref[...])
    i = jax.nn.sigmoid(_dot3(xc, wx_ref[...]) + bx_ref[...])
    log_a = -LRU_C * r * sp_ref[...]
    a = jnp.exp(log_a)
    u = jnp.sqrt(-jnp.tanh(log_a) * (a * a + 1.0)) * (i * xc)
    return a, u


def _lru_prompt_kernel(x_ref, gate_ref, cw_ref, cb_ref, wa_ref, ba_ref, wx_ref, bx_ref, sp_ref,
                       y_ref, hfin_ref, h_scr, tail_scr, a_scr, u_scr, hs_scr, *, T):
    c = pl.program_id(1)

    @pl.when(c == 0)
    def _():
        h_scr[...] = jnp.zeros_like(h_scr)
        tail_scr[...] = jnp.zeros_like(tail_scr)

    x = x_ref[...]
    xx = jnp.concatenate([tail_scr[...], x], axis=0)
    base = 8 - (CONV_W - 1)
    xc = cb_ref[...] + sum(xx[base + j:base + j + T] * cw_ref[j:j + 1, :] for j in range(CONV_W))
    tail_scr[...] = x[T - 8:T]
    a, u = _lru_gates(xc, wa_ref, ba_ref, wx_ref, bx_ref, sp_ref)
    a_scr[...] = a
    u_scr[...] = u

    def group(gi, h):
        off = pl.multiple_of(gi * 8, 8)
        ag = a_scr[pl.ds(off, 8), :]
        ug = u_scr[pl.ds(off, 8), :]
        outs = []
        for j in range(8):
            h = ag[j:j + 1] * h + ug[j:j + 1]
            outs.append(h)
        hs_scr[pl.ds(off, 8), :] = jnp.concatenate(outs, axis=0)
        return h

    h = lax.fori_loop(0, T // 8, group, h_scr[...])
    h_scr[...] = h
    hfin_ref[...] = h
    y_ref[...] = hs_scr[...] * jax.nn.gelu(gate_ref[...])


def _lru_weights(conv_w, conv_b, w_a, b_a, w_x, b_x, lam):
    def bd(w):
        eye = jnp.eye(LRU_HEADS, dtype=w.dtype)
        return (w[:, :, None, :] * eye[:, None, :, None]).reshape(LRU_WIDTH, LRU_WIDTH)
    sp = jax.nn.softplus(-lam.astype(F32)).reshape(1, LRU_WIDTH)
    r1 = lambda t: t.reshape(1, LRU_WIDTH)
    return conv_w, r1(conv_b), bd(w_a), r1(b_a), bd(w_x), r1(b_x), sp


def _lru_prompt_call(lru, weights, T):
    bsz, L, _ = lru.shape
    W = LRU_WIDTH
    const = lambda shape: pl.BlockSpec(shape, lambda b, c: (0,) * len(shape))
    kern = functools.partial(_lru_prompt_kernel, T=T)
    return pl.pallas_call(
        kern,
        grid=(bsz, L // T),
        in_specs=[pl.BlockSpec((None, T, W), lambda b, c: (b, c, 0)),
                  pl.BlockSpec((None, T, W), lambda b, c: (b, c, 1)),
                  const((CONV_W, W)), const((1, W)), const((W, W)), const((1, W)),
                  const((W, W)), const((1, W)), const((1, W))],
        out_specs=[pl.BlockSpec((None, T, W), lambda b, c: (b, c, 0)),
                   pl.BlockSpec((None, 1, W), lambda b, c: (b, 0, 0))],
        out_shape=[jax.ShapeDtypeStruct((bsz, L, W), F32), jax.ShapeDtypeStruct((bsz, 1, W), F32)],
        scratch_shapes=[pltpu.VMEM((1, W), F32), pltpu.VMEM((8, W), F32), pltpu.VMEM((T, W), F32),
                        pltpu.VMEM((T, W), F32), pltpu.VMEM((T, W), F32)],
        compiler_params=_cparams(("parallel", "arbitrary")),
        name="lru_prompt",
    )(lru, lru, *weights)


def _lru_sample_kernel(x_ref, gate_ref, buf_ref, h0_ref, cw_ref, cb_ref, wa_ref, ba_ref, wx_ref, bx_ref, sp_ref,
                       y_ref, hfin_ref, *, L):
    xx = [buf_ref[j] for j in range(CONV_W - 1)] + [x_ref[t] for t in range(L)]
    h = h0_ref[...]
    for t in range(L):
        xc = cb_ref[...] + sum(xx[t + j] * cw_ref[j:j + 1, :] for j in range(CONV_W))
        a, u = _lru_gates(xc, wa_ref, ba_ref, wx_ref, bx_ref, sp_ref)
        h = a * h + u
        y_ref[t] = h * jax.nn.gelu(gate_ref[t])
    hfin_ref[...] = h


def _lru_sample_call(x_t, gate_t, buf_t, h0, weights):
    L, bsz, W = x_t.shape
    kern = functools.partial(_lru_sample_kernel, L=L)
    return pl.pallas_call(
        kern,
        out_shape=[jax.ShapeDtypeStruct((L, bsz, W), F32), jax.ShapeDtypeStruct((bsz, W), F32)],
        compiler_params=pltpu.CompilerParams(vmem_limit_bytes=VMEM_LIMIT),
        name="lru_sample",
    )(x_t, gate_t, buf_t, h0, *weights)


def _later_and_total(tk=ATT_BLK):
    later = np.arange(tk)[:, None] > np.arange(tk)[None, :]
    return jnp.asarray(np.concatenate([later, np.ones((tk, LANES), bool)], axis=1), BF16)


def _sb_logs(z, uo, mask):
    tk = z.shape[1]
    sp = _softplus(z)
    ls = -sp if mask is None else jnp.where(mask, -sp, 0.0)
    hi, lo = _split2(ls)
    res = _dg(hi, uo) + _dg(lo, uo)
    return z - sp, res[:, :tk], res[:, tk:]


def _head_rows(q, n_rows):
    headmask = (_iota((SB_HEADS * n_rows, HW), 0) // n_rows) == (_iota((SB_HEADS * n_rows, HW), 1) // HEAD_DIM)
    qs = jnp.where(headmask, jnp.concatenate([q * (HEAD_DIM ** -0.5)] * SB_HEADS, axis=0), 0.0)
    return qs.astype(BF16), headmask


def _pick_heads(acc, headmask, n_rows):
    return sum(jnp.where(headmask[h * n_rows:(h + 1) * n_rows], acc[h * n_rows:(h + 1) * n_rows], 0.0)
               for h in range(SB_HEADS))


def _sb_prompt_kernel(q_ref, kv_ref, uo_ref, uo2_ref, o_ref, kvb_scr, acc_scr, cs_scr):
    i = pl.program_id(1)
    Q = ATT_BLK
    R = SB_HEADS * Q

    @pl.when(i == 0)
    def _():
        kvb_scr[...] = kv_ref[...].astype(BF16)

    qs, headmask = _head_rows(q_ref[...], Q)

    def tile(start, tk, uo_ref_, mask):
        kv = kvb_scr[pl.ds(pl.multiple_of(start, Q), tk), :]
        z = _dg(qs, kv[:, :HW], NT)
        zl, btw, tot = _sb_logs(z, uo_ref_[...], mask)
        cs = cs_scr[...]
        e = jnp.exp(zl + (jnp.concatenate([cs] * (tk // LANES), axis=1) + btw))
        w = e if mask is None else jnp.where(mask, e, 0.0)
        cs_scr[...] = cs + tot
        acc_scr[...] += _dg(w.astype(BF16), kv[:, HW:])

    acc_scr[...] = jnp.zeros_like(acc_scr)
    cs_scr[...] = jnp.zeros_like(cs_scr)
    tile(i * Q, Q, uo_ref, _iota((R, Q), 1) < _iota((R, Q), 0) % Q)

    @pl.when(i % 2 == 1)
    def _():
        tile((i - 1) * Q, Q, uo_ref, None)

    n_pairs = i // 2

    def body(jj, c):
        tile((n_pairs - 1 - jj) * 2 * Q, 2 * Q, uo2_ref, None)
        return c

    lax.fori_loop(0, n_pairs, body, 0)
    o_ref[...] = _pick_heads(acc_scr[...], headmask, Q)


def _sb_prompt_call(sbq, sbkv):
    bsz, L, _ = sbq.shape
    R = SB_HEADS * ATT_BLK
    return pl.pallas_call(
        _sb_prompt_kernel,
        grid=(bsz, L // ATT_BLK),
        in_specs=[pl.BlockSpec((None, ATT_BLK, HW), lambda b, i: (b, i, 0)),
                  pl.BlockSpec((None, L, 2 * HW), lambda b, i: (b, 0, 0)),
                  pl.BlockSpec((ATT_BLK, ATT_BLK + LANES), lambda b, i: (0, 0)),
                  pl.BlockSpec((2 * ATT_BLK, 2 * ATT_BLK + LANES), lambda b, i: (0, 0))],
        out_specs=pl.BlockSpec((None, ATT_BLK, HW), lambda b, i: (b, i, 0)),
        out_shape=jax.ShapeDtypeStruct((bsz, L, HW), F32),
        scratch_shapes=[pltpu.VMEM((L, 2 * HW), BF16), pltpu.VMEM((R, HW), F32), pltpu.VMEM((R, ATT_BLK), F32)],
        compiler_params=_cparams(("parallel", "arbitrary")),
        name="sb_prompt",
    )(sbq, sbkv, _later_and_total(), _later_and_total(2 * ATT_BLK))


def _sb_sample_kernel(pt_ref, q_ref, new_ref, *rest, L, pps):
    page_refs = rest[:pps]
    uo_ref, o_ref, acc_scr, cs_scr = rest[pps:]
    s = pl.program_id(1)
    R = SB_HEADS * L
    uo = uo_ref[...]
    q16, headmask = _head_rows(q_ref[...], L)

    @pl.when(s == 0)
    def _():
        new = new_ref[...]
        mask = _iota((R, ATT_BLK), 1) < (_iota((R, ATT_BLK), 0) % L)
        zl, btw, tot = _sb_logs(_dg(q16, new[:, :HW].astype(BF16), NT), uo, mask)
        w = jnp.where(mask, jnp.exp(zl + btw), 0.0)
        cs_scr[...] = tot
        acc_scr[...] = _dg(w.astype(BF16), new[:, HW:].astype(BF16))

    z = jnp.concatenate([_dg(q16, page_refs[r][:HW, :].astype(BF16)) for r in range(pps)], axis=0)
    zl, btw, tot = _sb_logs(z, uo, None)
    cs = cs_scr[...]
    acc = acc_scr[...]
    for r in range(pps):
        sl = slice(r * R, (r + 1) * R)
        w = jnp.exp(zl[sl] + (cs + btw[sl]))
        acc = acc + _dg(w.astype(BF16), page_refs[r][HW:, :].astype(BF16), NT)
        cs = cs + tot[sl]
    acc_scr[...] = acc
    cs_scr[...] = cs

    @pl.when(s == pl.num_programs(1) - 1)
    def _():
        o_ref[...] = _pick_heads(acc, headmask, L)


def _sb_sample_call(sbq, new_kv, cache_t, page_table, layer, pps):
    bsz, L, _ = sbq.shape
    n_pages = page_table.shape[1]
    steps = n_pages // pps
    R = SB_HEADS * L
    kern = functools.partial(_sb_sample_kernel, L=L, pps=pps)

    def page_spec(r):
        return pl.BlockSpec((None, None, 2 * HW, ATT_BLK),
                            lambda b, s, pt: (layer, pt[b, n_pages - 1 - (s * pps + r)], 0, 0))

    grid_spec = pltpu.PrefetchScalarGridSpec(
        num_scalar_prefetch=1,
        grid=(bsz, steps),
        in_specs=[pl.BlockSpec((None, L, HW), lambda b, s, pt: (b, 0, 0)),
                  pl.BlockSpec((None, ATT_BLK, 2 * HW), lambda b, s, pt: (b, 0, 0))]
                 + [page_spec(r) for r in range(pps)]
                 + [pl.BlockSpec((ATT_BLK, 2 * ATT_BLK), lambda b, s, pt: (0, 0))],
        out_specs=pl.BlockSpec((None, L, HW), lambda b, s, pt: (b, 0, 0)),
        scratch_shapes=[pltpu.VMEM((R, HW), F32), pltpu.VMEM((R, ATT_BLK), F32)],
    )
    return pl.pallas_call(
        kern,
        grid_spec=grid_spec,
        out_shape=jax.ShapeDtypeStruct((bsz, L, HW), F32),
        compiler_params=_cparams(("parallel", "arbitrary")),
        name="sb_sample",
    )(page_table, sbq, new_kv, *([cache_t] * pps), _later_and_total())


def _masked_softmax(s, valid):
    sm = jnp.where(valid, s, NEG)
    e = jnp.where(valid, jnp.exp(sm - jnp.max(sm, axis=1, keepdims=True)), 0.0)
    return e / jnp.maximum(jnp.sum(e, axis=1, keepdims=True), 1e-30)


def _top_select(score, n_sel, floor):
    lane = _iota(score.shape, 1)
    big = score.shape[1]
    sel = jnp.zeros(score.shape, F32)
    for _ in range(n_sel):
        m = jnp.max(score, axis=1, keepdims=True)
        idx = jnp.min(jnp.where(score == m, lane, big), axis=1, keepdims=True)
        hit = lane == idx
        sel = jnp.where(hit, 1.0, sel)
        score = jnp.where(hit, floor, score)
    return sel


def _flash_step(s, valid, vb, m_ref, acc_ref):
    sm = jnp.where(valid, s, NEG)
    m_old = m_ref[...]
    m_new = jnp.maximum(m_old, jnp.max(sm, axis=1, keepdims=True))
    p = jnp.exp(sm - m_new)
    acc_ref[...] = jnp.exp(m_old - m_new) * acc_ref[...] + _dg(p.astype(BF16), vb)
    m_ref[...] = m_new


def _nsa_prompt_kernel(q_ref, rows_ref, win_ref, misc_ref, ex_ref, o_ref,
                       kcvc_scr, kse_scr, vse_scr, kwe_scr, vwe_scr, score_scr, pick_scr, m_scr, acc_scr, *, L):
    i = pl.program_id(1)
    nb = L // NSA_BLOCK
    Q = ATT_BLK
    G = NSA_HEADS
    R = G * Q

    @pl.when(i == 0)
    def _():
        blk = rows_ref[:, :2 * HEAD_DIM].reshape(nb, NSA_BLOCK, 2 * HEAD_DIM)
        kcvc_scr[...] = jnp.sum(blk, axis=1) * (1.0 / NSA_BLOCK)
        ksvs = rows_ref[:, 2 * HEAD_DIM:]
        kse_scr[...] = _low_half(ksvs).astype(BF16)
        vse_scr[...] = _low_half(_swap_halves(ksvs), 1.0).astype(BF16)
        kwvw = win_ref[...]
        kwe_scr[...] = _low_half(kwvw).astype(BF16)
        vwe_scr[...] = _low_half(_swap_halves(kwvw), 1.0).astype(BF16)

    q4 = (_heads_to_rows(q_ref[...]) * (HEAD_DIM ** -0.5)).astype(BF16)
    qpos1 = i * Q + _iota((Q, 1), 0)
    qpos4 = jnp.concatenate([qpos1] * G, axis=0)

    kcvc = kcvc_scr[...]
    s_c = _dg(q4, _low_half(kcvc).astype(BF16), NT)
    valid_c = (_iota((R, nb), 1) + 1) * NSA_BLOCK - 1 <= qpos4
    p_c = _masked_softmax(s_c, valid_c)
    o_c = _dg(p_c.astype(BF16), _swap_halves(kcvc).astype(BF16))

    imp = sum(p_c[h * Q:(h + 1) * Q] for h in range(G))
    blk1 = _iota((Q, nb), 1)
    score = jnp.where(blk1 == qpos1 // NSA_BLOCK, NSA_FORCE, jnp.where(blk1 * NSA_BLOCK <= qpos1, imp, -1.0))
    score_t = score.T
    score_scr[...] = score_t
    nidx = _iota((nb, Q), 0)

    def rank_body(m, rank):
        row = score_scr[pl.ds(m, 1), :]
        earlier = jnp.where(nidx > m, 1.0, 0.0)
        return rank + jnp.where(row > score_t, 1.0, jnp.where(row == score_t, earlier, 0.0))

    rank = lax.fori_loop(0, nb, rank_body, jnp.zeros((nb, Q), F32), unroll=8)
    sel_t = jnp.where(rank < min(NSA_TOPK, nb), 1.0, 0.0).astype(BF16)
    picked = _dg(sel_t, ex_ref[...], TN)
    causal = _iota((Q, L), 1) <= qpos1
    pick_scr[...] = jnp.where(causal, picked, 0.0)

    m_scr[...] = jnp.full(m_scr.shape, NEG, F32)
    acc_scr[...] = jnp.zeros_like(acc_scr)
    n_tiles = (i * Q + Q + SEL_TILE - 1) // SEL_TILE

    def sel_body(jj, c):
        off = pl.multiple_of((n_tiles - 1 - jj) * SEL_TILE, SEL_TILE)
        valid1 = pick_scr[:, pl.ds(off, SEL_TILE)] > 0.5
        valid = jnp.concatenate([valid1] * G, axis=0)
        s = _dg(q4, kse_scr[pl.ds(off, SEL_TILE), :], NT)
        _flash_step(s, valid, vse_scr[pl.ds(off, SEL_TILE), :], m_scr, acc_scr)
        return c

    lax.fori_loop(0, n_tiles, sel_body, 0)
    acc = acc_scr[...]
    o_s = acc / jnp.maximum(_swap_halves(acc), 1e-30)

    m_scr[...] = jnp.full(m_scr.shape, NEG, F32)
    acc_scr[...] = jnp.zeros_like(acc_scr)
    col = _iota((Q, ATT_BLK), 1)
    for t in range(NSA_WINDOW // ATT_BLK + 1):
        j = i - t
        off = pl.multiple_of(jnp.maximum(j, 0) * ATT_BLK, ATT_BLK)
        kpos = j * ATT_BLK + col
        dist = qpos1 - kpos
        valid1 = (dist >= 0) & (dist <= NSA_WINDOW) & (kpos >= 0)
        valid = jnp.concatenate([valid1] * G, axis=0)
        s = _dg(q4, kwe_scr[pl.ds(off, ATT_BLK), :], NT)
        _flash_step(s, valid, vwe_scr[pl.ds(off, ATT_BLK), :], m_scr, acc_scr)
    acc = acc_scr[...]
    o_w = acc / jnp.maximum(_swap_halves(acc), 1e-30)

    gates = jax.nn.sigmoid(misc_ref[:, GLA_RANK:GLA_RANK + 3 * G])
    parts = []
    for h in range(G):
        sl = slice(h * Q, (h + 1) * Q)
        parts.append(gates[:, 3 * h:3 * h + 1] * o_c[sl] + gates[:, 3 * h + 1:3 * h + 2] * o_s[sl]
                     + gates[:, 3 * h + 2:3 * h + 3] * o_w[sl])
    o_ref[...] = _rows_to_heads(parts)


def _nsa_prompt_call(nq, rows, win, misc):
    bsz, L, _ = nq.shape
    nb = L // NSA_BLOCK
    R = NSA_HEADS * ATT_BLK
    expand = jnp.asarray(np.arange(nb)[:, None] == np.arange(L)[None, :] // NSA_BLOCK, BF16)
    kern = functools.partial(_nsa_prompt_kernel, L=L)
    return pl.pallas_call(
        kern,
        grid=(bsz, L // ATT_BLK),
        in_specs=[pl.BlockSpec((None, ATT_BLK, HW), lambda b, i: (b, i, 0)),
                  pl.BlockSpec((None, L, HW), lambda b, i: (b, 0, 0)),
                  pl.BlockSpec((None, L, 2 * HEAD_DIM), lambda b, i: (b, 0, 0)),
                  pl.BlockSpec((None, ATT_BLK, LANES), lambda b, i: (b, i, 0)),
                  pl.BlockSpec((nb, L), lambda b, i: (0, 0))],
        out_specs=pl.BlockSpec((None, ATT_BLK, HW), lambda b, i: (b, i, 0)),
        out_shape=jax.ShapeDtypeStruct((bsz, L, HW), F32),
        scratch_shapes=[pltpu.VMEM((nb, LANES), F32),
                        pltpu.VMEM((L, LANES), BF16), pltpu.VMEM((L, LANES), BF16),
                        pltpu.VMEM((L, LANES), BF16), pltpu.VMEM((L, LANES), BF16),
                        pltpu.VMEM((nb, ATT_BLK), F32), pltpu.VMEM((ATT_BLK, L), F32),
                        pltpu.VMEM((R, 1), F32), pltpu.VMEM((R, LANES), F32)],
        compiler_params=_cparams(("parallel", "arbitrary")),
        name="nsa_prompt",
    )(nq, rows, win, misc, expand)


def _nsa_sample_kernel(pt_ref, q_ref, new_ref, winp_ref, winn_ref, misc_ref, ea_ref, eb_ref, *rest,
                       L, pps, n_pages):
    page_refs = rest[:pps]
    o_ref, kcvc_scr, sel_scr, oc_scr, ow_scr, m_scr, l_scr, acc_scr = rest[pps:]
    s = pl.program_id(1)
    steps = n_pages // pps
    G = NSA_HEADS
    R = G * L
    per_page = ATT_BLK // NSA_BLOCK
    bps = pps * per_page
    nbp = n_pages * per_page
    nbp_pad = kcvc_scr.shape[1]
    q4 = (_heads_to_rows(q_ref[...]) * (HEAD_DIM ** -0.5)).astype(BF16)
    t_row = _iota((R, 1), 0) % L

    @pl.when(s == 0)
    def _():
        kcvc_scr[...] = jnp.zeros_like(kcvc_scr)

    @pl.when(s < steps)
    def _():
        x = jnp.concatenate([page_refs[r][...] for r in range(pps)], axis=1)
        means = _dot_xl(x, ea_ref[...])
        base = (s * bps) % LANES
        off = pl.multiple_of(((s * bps) // LANES) * LANES, LANES)
        kcvc_scr[:, pl.ds(off, LANES)] += pltpu.roll(means, base, 1)

    @pl.when(s == steps - 1)
    def _():
        kcvc = kcvc_scr[...].astype(BF16)
        live = _iota((R, nbp_pad), 1) < nbp
        s_c = _dg(q4, kcvc)
        p_c = _masked_softmax(s_c, live)
        oc_scr[...] = _swap_halves(_dg(p_c.astype(BF16), kcvc, NT))
        imp = sum(p_c[h * L:(h + 1) * L] for h in range(G))
        imp = jnp.where(live[:L], imp, -3.0)
        sel = _top_select(imp, min(NSA_TOPK, nbp + 1) - 1, -2.0)
        sel = jnp.concatenate([sel] * G, axis=0)
        for g in range(steps):
            sel_scr[g] = sel[:, g * bps:(g + 1) * bps]
        n_buf = winp_ref.shape[1]
        winp = winp_ref[...].astype(BF16)
        winn = winn_ref[...]
        dist_p = t_row + n_buf - _iota((R, n_buf), 1)
        dist_n = t_row - _iota((R, ATT_BLK), 1)
        s_w = jnp.concatenate([_dg(q4, winp), _dg(q4, _low_half(winn).astype(BF16), NT)], axis=1)
        valid_w = jnp.concatenate([(dist_p >= 0) & (dist_p <= NSA_WINDOW),
                                   (dist_n >= 0) & (dist_n <= NSA_WINDOW)], axis=1)
        p_w = _masked_softmax(s_w, valid_w).astype(BF16)
        ow_scr[...] = _swap_halves(_dg(p_w[:, :n_buf], winp, NT) + _dg(p_w[:, n_buf:], winn.astype(BF16)))
        ksvs = new_ref[:, 2 * HEAD_DIM:]
        sm = jnp.where(dist_n >= 0, _dg(q4, _low_half(ksvs).astype(BF16), NT), NEG)
        m0 = jnp.max(sm, axis=1, keepdims=True)
        p0 = jnp.exp(sm - m0)
        m_scr[...] = m0
        l_scr[...] = jnp.sum(p0, axis=1, keepdims=True)
        acc_scr[...] = _dg(p0.astype(BF16), ksvs.astype(BF16))

    @pl.when(s >= steps)
    def _():
        picked = _dg(sel_scr[s - steps].astype(BF16), eb_ref[...]) > 0.5
        pages = [page_refs[r][...].astype(BF16) for r in range(pps)]
        sc = jnp.concatenate([_dg(q4, pg) for pg in pages], axis=1)
        sm = jnp.where(picked, sc, NEG)
        m_old = m_scr[...]
        m_new = jnp.maximum(m_old, jnp.max(sm, axis=1, keepdims=True))
        p = jnp.exp(sm - m_new)
        alpha = jnp.exp(m_old - m_new)
        pb = p.astype(BF16)
        pv = sum(_dg(pb[:, r * ATT_BLK:(r + 1) * ATT_BLK], pages[r], NT) for r in range(pps))
        l_scr[...] = alpha * l_scr[...] + jnp.sum(p, axis=1, keepdims=True)
        acc_scr[...] = alpha * acc_scr[...] + pv
        m_scr[...] = m_new

    @pl.when(s == 2 * steps - 1)
    def _():
        o_s = _swap_halves(acc_scr[...]) / jnp.maximum(l_scr[...], 1e-30)
        o_c = oc_scr[...]
        o_w = ow_scr[...]
        gates = jax.nn.sigmoid(misc_ref[:, GLA_RANK:GLA_RANK + 3 * G])
        parts = []
        for h in range(G):
            sl = slice(h * L, (h + 1) * L)
            parts.append(gates[:, 3 * h:3 * h + 1] * o_c[sl] + gates[:, 3 * h + 1:3 * h + 2] * o_s[sl]
                         + gates[:, 3 * h + 2:3 * h + 3] * o_w[sl])
        o_ref[...] = _rows_to_heads(parts)


def _nsa_sample_call(nq, new_rows, win_past_t, win_new, misc, cache_t, page_table, layer, pps):
    bsz, L, _ = nq.shape
    n_pages = page_table.shape[1]
    steps = n_pages // pps
    n_buf = win_past_t.shape[3]
    per_page = ATT_BLK // NSA_BLOCK
    bps = pps * per_page
    nbp = n_pages * per_page
    nbp_pad = -(-nbp // LANES) * LANES
    assert LANES % bps == 0
    R = NSA_HEADS * L
    keys = np.arange(pps * ATT_BLK)
    ea = jnp.asarray((keys[:, None] // NSA_BLOCK == np.arange(LANES)[None, :]) / float(NSA_BLOCK), BF16)
    eb = jnp.asarray(np.arange(bps)[:, None] == keys[None, :] // NSA_BLOCK, BF16)
    kern = functools.partial(_nsa_sample_kernel, L=L, pps=pps, n_pages=n_pages)

    def page_spec(r):
        return pl.BlockSpec((None, None, 2 * HEAD_DIM, ATT_BLK),
                            lambda b, s, pt: (layer, pt[b, (s % steps) * pps + r], s // steps, 0))

    cst = lambda shape: pl.BlockSpec(shape, lambda b, s, pt: (0,) * len(shape))
    grid_spec = pltpu.PrefetchScalarGridSpec(
        num_scalar_prefetch=1,
        grid=(bsz, 2 * steps),
        in_specs=[pl.BlockSpec((None, L, HW), lambda b, s, pt: (b, 0, 0)),
                  pl.BlockSpec((None, ATT_BLK, HW), lambda b, s, pt: (b, 0, 0)),
                  pl.BlockSpec((None, None, 2 * HEAD_DIM, n_buf), lambda b, s, pt: (layer, b, 0, 0)),
                  pl.BlockSpec((None, ATT_BLK, 2 * HEAD_DIM), lambda b, s, pt: (b, 0, 0)),
                  pl.BlockSpec((None, L, LANES), lambda b, s, pt: (b, 0, 0)),
                  cst((pps * ATT_BLK, LANES)), cst((bps, pps * ATT_BLK))]
                 + [page_spec(r) for r in range(pps)],
        out_specs=pl.BlockSpec((None, L, HW), lambda b, s, pt: (b, 0, 0)),
        scratch_shapes=[pltpu.VMEM((2 * HEAD_DIM, nbp_pad), F32), pltpu.VMEM((steps, R, bps), F32),
                        pltpu.VMEM((R, LANES), F32), pltpu.VMEM((R, LANES), F32),
                        pltpu.VMEM((R, 1), F32), pltpu.VMEM((R, 1), F32), pltpu.VMEM((R, LANES), F32)],
    )
    return pl.pallas_call(
        kern,
        grid_spec=grid_spec,
        out_shape=jax.ShapeDtypeStruct((bsz, L, HW), F32),
        compiler_params=_cparams(("parallel", "arbitrary")),
        name="nsa_sample",
    )(page_table, nq, new_rows, win_past_t, win_new, misc, ea, eb, *([cache_t] * pps))


def _outproj_kernel(ya_ref, yb_ref, yc_ref, yd_ref, x_ref, w_ref, g_ref, b_ref, o_ref):
    y = jnp.concatenate([ya_ref[...], yb_ref[...], yc_ref[...], yd_ref[...]], axis=1).astype(BF16)
    h = jnp.dot(y, w_ref[...], preferred_element_type=F32)
    o_ref[...] = _layer_norm(DN_ALPHA * x_ref[...] + h, g_ref[...], b_ref[...])


def _outproj_call(ya, yb, yc, yd, x, w, g, b, tm):
    n = x.shape[0]
    part = lambda: pl.BlockSpec((tm, HW), lambda i: (i, 0))
    return pl.pallas_call(
        _outproj_kernel,
        grid=(n // tm,),
        in_specs=[part(), part(), part(), part(),
                  pl.BlockSpec((tm, D_MODEL), lambda i: (i, 0)),
                  pl.BlockSpec((4 * HW, D_MODEL), lambda i: (0, 0)),
                  pl.BlockSpec((1, D_MODEL), lambda i: (0, 0)),
                  pl.BlockSpec((1, D_MODEL), lambda i: (0, 0))],
        out_specs=pl.BlockSpec((tm, D_MODEL), lambda i: (i, 0)),
        out_shape=jax.ShapeDtypeStruct((n, D_MODEL), F32),
        compiler_params=_cparams(("parallel",)),
        name="out_proj_ln",
    )(ya, yb, yc, yd, x, w, g.reshape(1, D_MODEL), b.reshape(1, D_MODEL))


def _route(logits):
    lane = _iota(logits.shape, 1)
    live = lane < N_EXPERTS
    lg = jnp.where(live, logits, NEG)
    e = jnp.where(live, jnp.exp(lg - jnp.max(lg, axis=1, keepdims=True)), 0.0)
    probs = e / jnp.sum(e, axis=1, keepdims=True)
    big = logits.shape[1]

    def top2(vals):
        w1 = jnp.max(vals, axis=1, keepdims=True)
        i1 = jnp.min(jnp.where(vals == w1, lane, big), axis=1, keepdims=True)
        rest = jnp.where(lane == i1, -2.0, vals)
        w2 = jnp.max(rest, axis=1, keepdims=True)
        i2 = jnp.min(jnp.where(rest == w2, lane, big), axis=1, keepdims=True)
        return w1, i1, w2, i2

    best = None
    g_sel = None
    for g in range(N_GROUPS):
        in_g = (lane // EXPERTS_PER_GROUP) == g
        w1, _, w2, _ = top2(jnp.where(in_g, probs, -1.0))
        tot = w1 + w2
        if best is None:
            best, g_sel = tot, jnp.zeros_like(tot, dtype=jnp.int32)
        else:
            upd = tot > best
            g_sel = jnp.where(upd, g, g_sel)
            best = jnp.where(upd, tot, best)
    in_grp = live & ((lane // EXPERTS_PER_GROUP) == g_sel)
    w1, i1, w2, i2 = top2(jnp.where(in_grp, probs, -1.0))
    den = w1 + w2
    return jnp.where(lane == i1, w1 / den, 0.0) + jnp.where(lane == i2, w2 / den, 0.0)


def _moe_kernel(x_ref, rw_ref, wg_ref, wu_ref, wd_ref, g_ref, b_ref, o_ref, acc_scr, comb_scr, xb_scr):
    e = pl.program_id(1)

    @pl.when(e == 0)
    def _():
        x = x_ref[...]
        xb_scr[...] = x.astype(BF16)
        comb_scr[...] = _route(_dot3(x, rw_ref[...]))
        acc_scr[...] = jnp.zeros_like(acc_scr)

    xb = xb_scr[...]
    hg = jnp.dot(xb, wg_ref[...], preferred_element_type=F32)
    hu = jnp.dot(xb, wu_ref[...], preferred_element_type=F32)
    comb = comb_scr[...]
    c = jnp.sum(jnp.where(_iota(comb.shape, 1) == e, comb, 0.0), axis=1, keepdims=True)
    h = (hg * jax.nn.sigmoid(hg)) * hu * c
    acc_scr[...] += jnp.dot(h.astype(BF16), wd_ref[...], preferred_element_type=F32)

    @pl.when(e == pl.num_programs(1) - 1)
    def _():
        o_ref[...] = _layer_norm(DN_ALPHA * x_ref[...] + acc_scr[...], g_ref[...], b_ref[...])


def _moe_call(x, rw, wg, wu, wd, g, b, tm):
    n = x.shape[0]
    return pl.pallas_call(
        _moe_kernel,
        grid=(n // tm, N_EXPERTS),
        in_specs=[pl.BlockSpec((tm, D_MODEL), lambda i, e: (i, 0)),
                  pl.BlockSpec((D_MODEL, LANES), lambda i, e: (0, 0)),
                  pl.BlockSpec((None, D_MODEL, D_EXPERT), lambda i, e: (e, 0, 0)),
                  pl.BlockSpec((None, D_MODEL, D_EXPERT), lambda i, e: (e, 0, 0)),
                  pl.BlockSpec((None, D_EXPERT, D_MODEL), lambda i, e: (e, 0, 0)),
                  pl.BlockSpec((1, D_MODEL), lambda i, e: (0, 0)),
                  pl.BlockSpec((1, D_MODEL), lambda i, e: (0, 0))],
        out_specs=pl.BlockSpec((tm, D_MODEL), lambda i, e: (i, 0)),
        out_shape=jax.ShapeDtypeStruct((n, D_MODEL), F32),
        scratch_shapes=[pltpu.VMEM((tm, D_MODEL), F32), pltpu.VMEM((tm, LANES), F32),
                        pltpu.VMEM((tm, D_MODEL), BF16)],
        compiler_params=_cparams(("parallel", "arbitrary")),
        name="moe_ln",
    )(x, rw, wg, wu, wd, g.reshape(1, D_MODEL), b.reshape(1, D_MODEL))


def _pad_rows(t, n):
    return jnp.pad(t, ((0, 0), (0, n - t.shape[1]), (0, 0)))


def _row_tile(n, pref):
    t = min(pref, n)
    while n % t:
        t //= 2
    return t


def _keys_minor(cache):
    d, p, rows = cache.shape[:3]
    return jnp.transpose(cache, (0, 1, 3, 4, 5, 2)).reshape(d, p, -1, rows)


def kernel(x_prompt, x_sample, cache_sb_kv, cache_nsa_kv, cache_nsa_win, state_gla, state_lru_h, state_lru_conv, page_table, ln_in_g, ln_in_b, w_in, gla_w_a2, gla_b_a, gla_norm_g, lru_conv_w, lru_conv_b, lru_w_a, lru_b_a, lru_w_x, lru_b_x, lru_lambda, w_out, ln1_g, ln1_b, router_w, moe_w_gate, moe_w_up, moe_w_down, ln2_g, ln2_b):
    bp, lp, d = x_prompt.shape
    bs, ls, _ = x_sample.shape
    depth = w_in.shape[0]
    page = cache_sb_kv.shape[2]
    n_pages = page_table.shape[1]
    past_len = n_pages * page
    n_buf = cache_nsa_win.shape[2]
    n_p, n_s = bp * lp, bs * ls
    assert page == ATT_BLK and lp % SEL_TILE == 0 and ls <= 8 and d == D_MODEL
    assert n_pages * (ATT_BLK // NSA_BLOCK) >= NSA_TOPK - 1 and n_buf <= past_len

    tm_p = _row_tile(n_p, 512)
    tm_s = _row_tile(n_s, 512)
    tab_p = _rope_table(jnp.arange(lp))
    tab_s = jnp.tile(_rope_table(past_len + jnp.arange(ls)), (bs, 1))
    assert lp % tm_p == 0 and tm_s == n_s

    cache_sb_t = _keys_minor(cache_sb_kv)
    cache_nsa_t = _keys_minor(cache_nsa_kv)
    win_past_t = _keys_minor(cache_nsa_win)
    win_past = cache_nsa_win.reshape(depth, bs, n_buf, 2 * HEAD_DIM)
    rw = jnp.pad(router_w, ((0, 0), (0, LANES - N_EXPERTS)))
    pps = 16
    while n_pages % pps:
        pps //= 2
    gla_c = 64 if lp % 64 == 0 else ATT_BLK
    lru_t = _row_tile(lp, 512)

    xp = _ln_call(x_prompt.reshape(n_p, d), ln_in_g, ln_in_b, tm_p)
    xs = _ln_call(x_sample.reshape(n_s, d), ln_in_g, ln_in_b, tm_s)

    st_p, st_s = [], []
    for l in range(depth):
        w_l = _prep_w_in(w_in[l])
        lru_w = _lru_weights(lru_conv_w[l], lru_conv_b[l], lru_w_a[l], lru_b_a[l], lru_w_x[l], lru_b_x[l],
                             lru_lambda[l])
        w_out_l = w_out[l].astype(BF16)
        wg, wu, wd = moe_w_gate[l].astype(BF16), moe_w_up[l].astype(BF16), moe_w_down[l].astype(BF16)

        gla, lru, sbq, sbkv, nq, nsa, win, misc = _proj_call(xp, w_l, tab_p, tm_p)
        r3 = lambda t, b_, l_: t.reshape(b_, l_, t.shape[-1])
        gla3, lru3, sbq3, sbkv3 = r3(gla, bp, lp), r3(lru, bp, lp), r3(sbq, bp, lp), r3(sbkv, bp, lp)
        nq3, nsa3, win3, misc3 = r3(nq, bp, lp), r3(nsa, bp, lp), r3(win, bp, lp), r3(misc, bp, lp)
        y_a, gla_t = _gla_call(gla3, misc3, gla_w_a2[l], gla_b_a[l], gla_norm_g[l],
                               jnp.zeros((bp, GV_W, QK_W), F32), gla_c, gla_c)
        y_b, lru_h = _lru_prompt_call(lru3, lru_w, lru_t)
        y_c = _sb_prompt_call(sbq3, sbkv3)
        y_d = _nsa_prompt_call(nq3, nsa3, win3, misc3)
        x1 = _outproj_call(y_a.reshape(n_p, HW), y_b.reshape(n_p, HW), y_c.reshape(n_p, HW), y_d.reshape(n_p, HW),
                           xp, w_out_l, ln1_g[l], ln1_b[l], tm_p)
        xp = _moe_call(x1, rw, wg, wu, wd, ln2_g[l], ln2_b[l], _row_tile(n_p, 1024))
        wn = min(NSA_WINDOW, lp)
        st_p.append((sbkv3.reshape(bp, lp, 2, SB_HEADS, HEAD_DIM), nsa3.reshape(bp, lp, 4, 1, HEAD_DIM),
                     win3[:, lp - wn:].reshape(bp, wn, 2, 1, HEAD_DIM), _gla_state_from_t(gla_t),
                     lru_h.reshape(bp, LRU_WIDTH), lru3[:, lp - (CONV_W - 1):, :LRU_WIDTH]))

        gla, lru, sbq, sbkv, nq, nsa, win, misc = _proj_call(xs, w_l, tab_s, tm_s)
        gla3, lru3, sbq3, sbkv3 = r3(gla, bs, ls), r3(lru, bs, ls), r3(sbq, bs, ls), r3(sbkv, bs, ls)
        nq3, nsa3, win3, misc3 = r3(nq, bs, ls), r3(nsa, bs, ls), r3(win, bs, ls), r3(misc, bs, ls)
        gc = 16
        y_a, gla_t = _gla_call(_pad_rows(gla3, gc), _pad_rows(misc3, gc), gla_w_a2[l], gla_b_a[l], gla_norm_g[l],
                               _gla_state_to_t(state_gla[l].astype(F32)), gc, ls)
        y_a = y_a[:, :ls]
        tmaj = lambda t: jnp.swapaxes(t, 0, 1)
        y_b, lru_h = _lru_sample_call(tmaj(lru3[:, :, :LRU_WIDTH]), tmaj(lru3[:, :, LRU_WIDTH:]),
                                      tmaj(state_lru_conv[l]), state_lru_h[l].astype(F32), lru_w)
        y_b = tmaj(y_b)
        y_c = _sb_sample_call(sbq3, _pad_rows(sbkv3, ATT_BLK), cache_sb_t, page_table, l, pps)
        y_d = _nsa_sample_call(nq3, _pad_rows(nsa3, ATT_BLK), win_past_t, _pad_rows(win3, ATT_BLK), misc3,
                               cache_nsa_t, page_table, l, pps)
        x1 = _outproj_call(y_a.reshape(n_s, HW), y_b.reshape(n_s, HW), y_c.reshape(n_s, HW), y_d.reshape(n_s, HW),
                           xs, w_out_l, ln1_g[l], ln1_b[l], tm_s)
        xs = _moe_call(x1, rw, wg, wu, wd, ln2_g[l], ln2_b[l], tm_s)
        win_all = jnp.concatenate([win_past[l], win3], axis=1)
        lru_buf = jnp.concatenate([state_lru_conv[l], lru3[:, :, :LRU_WIDTH]], axis=1)[:, ls:]
        st_s.append((sbkv3.reshape(bs, ls, 2, SB_HEADS, HEAD_DIM), nsa3.reshape(bs, ls, 4, 1, HEAD_DIM),
                     win_all[:, win_all.shape[1] - n_buf:].reshape(bs, n_buf, 2, 1, HEAD_DIM),
                     _gla_state_from_t(gla_t), lru_h, lru_buf))

    outs_p = [jnp.stack([st[i] for st in st_p]) for i in range(6)]
    outs_s = [jnp.stack([st[i] for st in st_s]) for i in range(6)]
    res = [xp.reshape(bp, lp, d), xs.reshape(bs, ls, d)]
    for a, b in zip(outs_p, outs_s):
        res += [a, b]
    return tuple(res)
```

```python
import functools

import jax
import jax.numpy as jnp
import numpy as np
from jax import lax
from jax.experimental import pallas as pl
from jax.experimental.pallas import tpu as pltpu

F32 = jnp.float32
BF16 = jnp.bfloat16

D_MODEL = 1024
HEAD_DIM = 64
GLA_HEADS, GLA_DK, GLA_DV, GLA_RANK, GLA_TAU = 4, 32, 64, 16, 16.0
LRU_WIDTH, LRU_HEADS, CONV_W, LRU_C = 256, 4, 4, 8.0
LRU_BLOCK = LRU_WIDTH // LRU_HEADS
SB_HEADS = 4
NSA_HEADS, NSA_BLOCK, NSA_TOPK, NSA_WINDOW, NSA_FORCE = 4, 64, 16, 512, 1.0e4
ROPE_THETA, ROPE_DIM = 500000.0, HEAD_DIM // 4
N_EXPERTS, N_GROUPS, TOP_K, D_EXPERT = 16, 4, 2, 512
EXPERTS_PER_GROUP = N_EXPERTS // N_GROUPS
DEPTH = 2
DN_ALPHA = (2.0 * DEPTH) ** 0.25
EPS = 1e-5
NEG = -1e30

QK_W = GLA_HEADS * GLA_DK
GV_W = GLA_HEADS * GLA_DV
HW = 4 * HEAD_DIM
IN_SIZES = (QK_W, QK_W, GV_W, GV_W, GLA_RANK, LRU_WIDTH, LRU_WIDTH, HW, HW, HW, HW) + (HEAD_DIM,) * 6 + (12,)
IN_OFFSETS = tuple(int(v) for v in np.cumsum(IN_SIZES)[:-1])
P_GLA, P_LRU, P_SBQ, P_SBKV, P_NQ, P_NSA, P_WIN, P_MISC = 0, 768, 1280, 1536, 2048, 2304, 2560, 2688
P_TOTAL = 2816
LANES = 128
ATT_BLK = 128
SEL_TILE = 4 * ATT_BLK
VMEM_LIMIT = 56 * 1024 * 1024

NN = (((1,), (0,)), ((), ()))
NT = (((1,), (1,)), ((), ()))
TN = (((0,), (0,)), ((), ()))


def _cparams(sem):
    return pltpu.CompilerParams(dimension_semantics=sem, vmem_limit_bytes=VMEM_LIMIT)


def _split2(a):
    hi = a.astype(BF16)
    lo = (a - hi.astype(F32)).astype(BF16)
    return hi, lo


def _split3(a):
    hi = a.astype(BF16)
    r = a - hi.astype(F32)
    mid = r.astype(BF16)
    lo = (r - mid.astype(F32)).astype(BF16)
    return hi, mid, lo


def _dg(a, b, dims=NN):
    return lax.dot_general(a, b, dims, preferred_element_type=F32)


def _dot1(a, b, dims=NN):
    return _dg(a.astype(BF16), b.astype(BF16), dims)


def _dot3(a, b, dims=NN):
    ah, al = _split2(a)
    bh, bl = _split2(b)
    return _dg(ah, bh, dims) + (_dg(ah, bl, dims) + _dg(al, bh, dims))


def _dot_xl(a, b_exact, dims=NN):
    h, m, l = _split3(a)
    return _dg(h, b_exact, dims) + (_dg(m, b_exact, dims) + _dg(l, b_exact, dims))


def _dot_lx(a_exact, b, dims=NN):
    h, m, l = _split3(b)
    return _dg(a_exact, h, dims) + (_dg(a_exact, m, dims) + _dg(a_exact, l, dims))


def _softplus(x):
    return jnp.maximum(x, 0.0) + jnp.log(1.0 + jnp.exp(-jnp.abs(x)))


def _log_sigmoid(x):
    return -_softplus(-x)


def _layer_norm(x, g, b):
    mu = jnp.mean(x, axis=-1, keepdims=True)
    xc = x - mu
    var = jnp.mean(xc * xc, axis=-1, keepdims=True)
    return xc * lax.rsqrt(var + EPS) * g + b


def _iota(shape, dim):
    return lax.broadcasted_iota(jnp.int32, shape, dim)


def _low_half(x, fill=0.0):
    return jnp.where(_iota(x.shape, 1) < HEAD_DIM, x, fill)


def _swap_halves(x):
    return pltpu.roll(x, HEAD_DIM, 1)


def _heads_to_rows(q):
    pieces = []
    for h in range(HW // HEAD_DIM):
        pair = q[:, (h // 2) * LANES:(h // 2 + 1) * LANES]
        pieces.append(_low_half(pair if h % 2 == 0 else _swap_halves(pair)))
    return jnp.concatenate(pieces, axis=0)


def _rows_to_heads(parts):
    pairs = [jnp.where(_iota(parts[0].shape, 1) < HEAD_DIM, parts[2 * p], _swap_halves(parts[2 * p + 1]))
             for p in range(len(parts) // 2)]
    return jnp.concatenate(pairs, axis=1)


def _ln_kernel(x_ref, g_ref, b_ref, o_ref):
    o_ref[...] = _layer_norm(x_ref[...], g_ref[...], b_ref[...])


def _ln_call(x, g, b, tm):
    n, d = x.shape
    return pl.pallas_call(
        _ln_kernel,
        grid=(n // tm,),
        in_specs=[pl.BlockSpec((tm, d), lambda i: (i, 0)),
                  pl.BlockSpec((1, d), lambda i: (0, 0)),
                  pl.BlockSpec((1, d), lambda i: (0, 0))],
        out_specs=pl.BlockSpec((tm, d), lambda i: (i, 0)),
        out_shape=jax.ShapeDtypeStruct((n, d), F32),
        compiler_params=_cparams(("parallel",)),
        name="ln_in",
    )(x, g.reshape(1, d), b.reshape(1, d))


def _rope(v, c, s, flags):
    ones = jnp.ones_like(c)
    zeros = jnp.zeros_like(s)
    cc = jnp.concatenate([c if f else ones for f in flags], axis=-1)
    ss = jnp.concatenate([s if f else zeros for f in flags], axis=-1)
    n = v.shape[-1]
    half = ROPE_DIM // 2
    lane = _iota(v.shape, 1) % HEAD_DIM
    sw = jnp.where(lane < half, pltpu.roll(v, n - half, 1), pltpu.roll(v, half, 1))
    return v * cc + sw * ss


def _proj_kernel(x_ref, w_ref, tab_ref, gla_ref, lru_ref, sbq_ref, sbkv_ref, nq_ref, nsa_ref, win_ref, misc_ref,
                 sbkvb_ref):
    x = x_ref[...].astype(BF16)

    def mm(off, width):
        return jnp.dot(x, w_ref[:, off:off + width], preferred_element_type=F32)

    gla_ref[...] = mm(P_GLA, 768)
    lru_ref[...] = mm(P_LRU, 512)
    sbq_ref[...] = mm(P_SBQ, 256)
    sbkv = mm(P_SBKV, 512)
    sbkv_ref[...] = sbkv
    sbkvb_ref[...] = sbkv.astype(BF16)
    tab = tab_ref[...]
    c = tab[:, :HEAD_DIM]
    s = tab[:, HEAD_DIM:]
    nq_ref[...] = _rope(mm(P_NQ, 256), c, s, (1, 1, 1, 1))
    nsa_ref[...] = _rope(mm(P_NSA, 256), c, s, (1, 0, 1, 0))
    win_ref[...] = _rope(mm(P_WIN, 128), c, s, (1, 0))
    misc_ref[...] = mm(P_MISC, 128)


def _proj_call(x, w, tab, tm):
    n = x.shape[0]
    tab_blocks = tab.shape[0] // tm
    widths = (768, 512, 256, 512, 256, 256, 128, 128)
    return pl.pallas_call(
        _proj_kernel,
        grid=(n // tm,),
        in_specs=[pl.BlockSpec((tm, D_MODEL), lambda i: (i, 0)),
                  pl.BlockSpec((D_MODEL, P_TOTAL), lambda i: (0, 0)),
                  pl.BlockSpec((tm, LANES), lambda i: (i % tab_blocks, 0))],
        out_specs=[pl.BlockSpec((tm, wd), lambda i: (i, 0)) for wd in widths + (512,)],
        out_shape=[jax.ShapeDtypeStruct((n, wd), F32) for wd in widths]
                  + [jax.ShapeDtypeStruct((n, 512), BF16)],
        compiler_params=_cparams(("parallel",)),
        name="in_proj",
    )(x, w, tab)


def _rope_table(pos):
    inv = ROPE_THETA ** (-jnp.arange(0, ROPE_DIM, 2, dtype=F32) / ROPE_DIM)
    ang = pos.astype(F32)[:, None] * inv[None, :]
    cos, sin = jnp.cos(ang), jnp.sin(ang)
    n = pos.shape[0]
    c = jnp.concatenate([cos, cos, jnp.ones((n, HEAD_DIM - ROPE_DIM), F32)], axis=-1)
    s = jnp.concatenate([-sin, sin, jnp.zeros((n, HEAD_DIM - ROPE_DIM), F32)], axis=-1)
    return jnp.concatenate([c, s], axis=-1)


def _prep_w_in(w):
    parts = jnp.split(w, IN_OFFSETS, axis=-1)
    (gq, gk, gv, gg, ga, lx, lg, sq, sk, sv, nq, nkc, nvc, nks, nvs, nkw, nvw, ngate) = parts
    pad = jnp.zeros((w.shape[0], LANES - GLA_RANK - 12), w.dtype)
    return jnp.concatenate([gq, gk, gv, gg, lx, lg, sq, sk, sv, nq, nkc, nvc, nks, nvs, nkw, nvw,
                            ga, ngate, pad], axis=-1).astype(BF16)


def _gla_kernel(q_ref, k_ref, v_ref, g_ref, misc_ref, wa2_ref, ba_ref, ng_ref, indv_ref, bd_ref, ones_ref,
                s0_ref, y_ref, st_ref, st_scr, b_scr, phi_scr, plo_scr, att_scr, *, C, Cv, BB):
    c = pl.program_id(1)

    @pl.when(c == 0)
    def _():
        st_scr[...] = s0_ref[...]

    rows = _iota((C, QK_W), 0)
    tril = (_iota((C, C), 0) >= _iota((C, C), 1)).astype(BF16)
    qs, bs = [], []
    for bi in range(BB):
        la = _log_sigmoid(_dot3(misc_ref[bi, :, :GLA_RANK], wa2_ref[...]) + ba_ref[...]) * (1.0 / GLA_TAU)
        if Cv < C:
            la = jnp.where(rows < Cv, la, 0.0)
        b = _dot_lx(tril, la)
        b_scr[bi] = b
        bs.append(b)
        qs.append(q_ref[bi] * (GLA_DK ** -0.5))

    def fill(s, carry):
        off = pl.multiple_of(s * C, C)
        for bi in range(BB):
            b_s = b_scr[bi, pl.ds(s, 1), :]
            k_s = k_ref[bi, pl.ds(s, 1), :]
            d = jnp.where(rows >= s, bs[bi] - b_s, -jnp.inf)
            hi, lo = _split2(qs[bi] * k_s * jnp.exp(d))
            phi_scr[bi, pl.ds(off, C), :] = hi
            plo_scr[bi, pl.ds(off, C), :] = lo
        return carry

    lax.fori_loop(0, Cv, fill, 0, unroll=2)
    indv = indv_ref[...]
    for bi in range(BB):
        att_scr[bi, :Cv * C, :] = _dg(phi_scr[bi, :Cv * C, :], indv) + _dg(plo_scr[bi, :Cv * C, :], indv)

    def gather(s, os):
        off = pl.multiple_of(s * C, C)
        return tuple(os[bi] + att_scr[bi, pl.ds(off, C), :] * v_ref[bi, pl.ds(s, 1), :] for bi in range(BB))

    os = lax.fori_loop(0, Cv, gather, tuple(jnp.zeros((C, GV_W), F32) for _ in range(BB)), unroll=2)
    for bi in range(BB):
        b = bs[bi]
        st = st_scr[bi]
        o = os[bi] + _dot3(qs[bi] * jnp.exp(b), st, NT)
        bl = b[C - 1:C, :]
        kd = k_ref[bi] * jnp.exp(bl - b)
        upd = _dot3(v_ref[bi], kd, TN) * bd_ref[...]
        st_new = st * jnp.exp(bl) + upd
        st_scr[bi] = st_new
        st_ref[bi] = st_new
        ms = _dot_xl(o * o, ones_ref[...])
        on = o * lax.rsqrt(ms + EPS) * ng_ref[...]
        g = g_ref[bi]
        y_ref[bi] = on * (g * jax.nn.sigmoid(g))


def _gla_call(gla, misc, wa2, ba, ng, s0_t, C, Cv):
    bsz, L, _ = gla.shape
    nc = L // C
    indv = (np.arange(QK_W)[:, None] // GLA_DK == np.arange(GV_W)[None, :] // GLA_DV)
    bd = (np.arange(GV_W)[:, None] // GLA_DV == np.arange(QK_W)[None, :] // GLA_DK)
    ones = (np.arange(GV_W)[:, None] // GLA_DV == np.arange(GV_W)[None, :] // GLA_DV) / float(GLA_DV)
    const = lambda shape: pl.BlockSpec(shape, lambda b, c: (0,) * len(shape))
    BB = _row_tile(bsz, 4)
    kern = functools.partial(_gla_kernel, C=C, Cv=Cv, BB=BB)
    return pl.pallas_call(
        kern,
        grid=(bsz // BB, nc),
        in_specs=[pl.BlockSpec((BB, C, QK_W), lambda b, c: (b, c, 0)),
                  pl.BlockSpec((BB, C, QK_W), lambda b, c: (b, c, 1)),
                  pl.BlockSpec((BB, C, GV_W), lambda b, c: (b, c, 1)),
                  pl.BlockSpec((BB, C, GV_W), lambda b, c: (b, c, 2)),
                  pl.BlockSpec((BB, C, LANES), lambda b, c: (b, c, 0)),
                  const((GLA_RANK, QK_W)), const((1, QK_W)), const((1, GV_W)),
                  const((QK_W, GV_W)), const((GV_W, QK_W)), const((GV_W, GV_W)),
                  pl.BlockSpec((BB, GV_W, QK_W), lambda b, c: (b, 0, 0))],
        out_specs=[pl.BlockSpec((BB, C, GV_W), lambda b, c: (b, c, 0)),
                   pl.BlockSpec((BB, GV_W, QK_W), lambda b, c: (b, 0, 0))],
        out_shape=[jax.ShapeDtypeStruct((bsz, L, GV_W), F32),
                   jax.ShapeDtypeStruct((bsz, GV_W, QK_W), F32)],
        scratch_shapes=[pltpu.VMEM((BB, GV_W, QK_W), F32), pltpu.VMEM((BB, C, QK_W), F32),
                        pltpu.VMEM((BB, C * C, QK_W), BF16), pltpu.VMEM((BB, C * C, QK_W), BF16),
                        pltpu.VMEM((BB, C * C, GV_W), F32)],
        compiler_params=_cparams(("parallel", "arbitrary")),
        name="gla",
    )(gla, gla, gla, gla, misc, wa2, ba.reshape(1, QK_W), ng.reshape(1, GV_W),
      jnp.asarray(indv, BF16), jnp.asarray(bd, F32), jnp.asarray(ones, BF16), s0_t)


def _gla_state_to_t(s0):
    bsz = s0.shape[0]
    st = jnp.swapaxes(s0, 2, 3)
    eye = jnp.eye(GLA_HEADS, dtype=s0.dtype)
    full = st[:, :, :, None, :] * eye[None, :, None, :, None]
    return full.reshape(bsz, GV_W, QK_W)


def _gla_state_from_t(st):
    bsz = st.shape[0]
    full = st.reshape(bsz, GLA_HEADS, GLA_DV, GLA_HEADS, GLA_DK)
    diag = jnp.stack([full[:, h, :, h, :] for h in range(GLA_HEADS)], axis=1)
    return jnp.swapaxes(diag, 2, 3)


def _lru_gates(xc, wa_ref, ba_ref, wx_ref, bx_ref, sp_ref):
    r = jax.nn.sigmoid(_dot3(xc, wa_ref[...]) + ba_ref[...])
    i = jax.nn.sigmoid(_dot3(xc, wx_ref[...]) + bx_ref[...])
    log_a = -LRU_C * r * sp_ref[...]
    a = jnp.exp(log_a)
    u = jnp.sqrt(-jnp.tanh(log_a) * (a * a + 1.0)) * (i * xc)
    return a, u


def _lru_prompt_kernel(x_ref, gate_ref, cw_ref, cb_ref, wa_ref, ba_ref, wx_ref, bx_ref, sp_ref,
                       y_ref, hfin_ref, h_scr, tail_scr, a_scr, u_scr, hs_scr, *, T):
    c = pl.program_id(1)

    @pl.when(c == 0)
    def _():
        h_scr[...] = jnp.zeros_like(h_scr)
        tail_scr[...] = jnp.zeros_like(tail_scr)

    x = x_ref[...]
    xx = jnp.concatenate([tail_scr[...], x], axis=0)
    base = 8 - (CONV_W - 1)
    xc = cb_ref[...] + sum(xx[base + j:base + j + T] * cw_ref[j:j + 1, :] for j in range(CONV_W))
    tail_scr[...] = x[T - 8:T]
    a, u = _lru_gates(xc, wa_ref, ba_ref, wx_ref, bx_ref, sp_ref)
    a_scr[...] = a
    u_scr[...] = u

    def group(gi, h):
        off = pl.multiple_of(gi * 8, 8)
        ag = a_scr[pl.ds(off, 8), :]
        ug = u_scr[pl.ds(off, 8), :]
        outs = []
        for j in range(8):
            h = ag[j:j + 1] * h + ug[j:j + 1]
            outs.append(h)
        hs_scr[pl.ds(off, 8), :] = jnp.concatenate(outs, axis=0)
        return h

    h = lax.fori_loop(0, T // 8, group, h_scr[...])
    h_scr[...] = h
    hfin_ref[...] = h
    y_ref[...] = hs_scr[...] * jax.nn.gelu(gate_ref[...])


def _lru_weights(conv_w, conv_b, w_a, b_a, w_x, b_x, lam):
    def bd(w):
        eye = jnp.eye(LRU_HEADS, dtype=w.dtype)
        return (w[:, :, None, :] * eye[:, None, :, None]).reshape(LRU_WIDTH, LRU_WIDTH)
    sp = jax.nn.softplus(-lam.astype(F32)).reshape(1, LRU_WIDTH)
    r1 = lambda t: t.reshape(1, LRU_WIDTH)
    return conv_w, r1(conv_b), bd(w_a), r1(b_a), bd(w_x), r1(b_x), sp


def _lru_prompt_call(lru, weights, T):
    bsz, L, _ = lru.shape
    W = LRU_WIDTH
    const = lambda shape: pl.BlockSpec(shape, lambda b, c: (0,) * len(shape))
    kern = functools.partial(_lru_prompt_kernel, T=T)
    return pl.pallas_call(
        kern,
        grid=(bsz, L // T),
        in_specs=[pl.BlockSpec((None, T, W), lambda b, c: (b, c, 0)),
                  pl.BlockSpec((None, T, W), lambda b, c: (b, c, 1)),
                  const((CONV_W, W)), const((1, W)), const((W, W)), const((1, W)),
                  const((W, W)), const((1, W)), const((1, W))],
        out_specs=[pl.BlockSpec((None, T, W), lambda b, c: (b, c, 0)),
                   pl.BlockSpec((None, 1, W), lambda b, c: (b, 0, 0))],
        out_shape=[jax.ShapeDtypeStruct((bsz, L, W), F32), jax.ShapeDtypeStruct((bsz, 1, W), F32)],
        scratch_shapes=[pltpu.VMEM((1, W), F32), pltpu.VMEM((8, W), F32), pltpu.VMEM((T, W), F32),
                        pltpu.VMEM((T, W), F32), pltpu.VMEM((T, W), F32)],
        compiler_params=_cparams(("parallel", "arbitrary")),
        name="lru_prompt",
    )(lru, lru, *weights)


def _lru_sample_kernel(x_ref, gate_ref, buf_ref, h0_ref, cw_ref, cb_ref, wa_ref, ba_ref, wx_ref, bx_ref, sp_ref,
                       y_ref, hfin_ref, *, L):
    xx = [buf_ref[j] for j in range(CONV_W - 1)] + [x_ref[t] for t in range(L)]
    h = h0_ref[...]
    for t in range(L):
        xc = cb_ref[...] + sum(xx[t + j] * cw_ref[j:j + 1, :] for j in range(CONV_W))
        a, u = _lru_gates(xc, wa_ref, ba_ref, wx_ref, bx_ref, sp_ref)
        h = a * h + u
        y_ref[t] = h * jax.nn.gelu(gate_ref[t])
    hfin_ref[...] = h


def _lru_sample_call(x_t, gate_t, buf_t, h0, weights):
    L, bsz, W = x_t.shape
    kern = functools.partial(_lru_sample_kernel, L=L)
    return pl.pallas_call(
        kern,
        out_shape=[jax.ShapeDtypeStruct((L, bsz, W), F32), jax.ShapeDtypeStruct((bsz, W), F32)],
        compiler_params=pltpu.CompilerParams(vmem_limit_bytes=VMEM_LIMIT),
        name="lru_sample",
    )(x_t, gate_t, buf_t, h0, *weights)


def _later_and_total(tk=ATT_BLK):
    later = np.arange(tk)[:, None] > np.arange(tk)[None, :]
    return jnp.asarray(np.concatenate([later, np.ones((tk, LANES), bool)], axis=1), BF16)


def _sb_logs(z, uo, mask):
    tk = z.shape[1]
    sp = _softplus(z)
    ls = -sp if mask is None else jnp.where(mask, -sp, 0.0)
    hi, lo = _split2(ls)
    res = _dg(hi, uo) + _dg(lo, uo)
    return z - sp, res[:, :tk], res[:, tk:]


def _head_rows(q, n_rows):
    headmask = (_iota((SB_HEADS * n_rows, HW), 0) // n_rows) == (_iota((SB_HEADS * n_rows, HW), 1) // HEAD_DIM)
    qs = jnp.where(headmask, jnp.concatenate([q * (HEAD_DIM ** -0.5)] * SB_HEADS, axis=0), 0.0)
    return qs.astype(BF16), headmask


def _pick_heads(acc, headmask, n_rows):
    return sum(jnp.where(headmask[h * n_rows:(h + 1) * n_rows], acc[h * n_rows:(h + 1) * n_rows], 0.0)
               for h in range(SB_HEADS))


def _sb_prompt_kernel(q_ref, kv_ref, uo_ref, uo2_ref, o_ref, acc_scr, cs_scr, *, BB):
    i = pl.program_id(1)
    Q = ATT_BLK
    R = SB_HEADS * Q
    heads = [_head_rows(q_ref[bi], Q) for bi in range(BB)]
    headmask = heads[0][1]

    def tile(start, tk, uo_ref_, mask):
        uo = uo_ref_[...]
        for bi in range(BB):
            kv = kv_ref[bi, pl.ds(pl.multiple_of(start, Q), tk), :]
            z = _dg(heads[bi][0], kv[:, :HW], NT)
            zl, btw, tot = _sb_logs(z, uo, mask)
            cs = cs_scr[bi]
            e = jnp.exp(zl + (jnp.concatenate([cs] * (tk // LANES), axis=1) + btw))
            w = e if mask is None else jnp.where(mask, e, 0.0)
            cs_scr[bi] = cs + tot
            acc_scr[bi] += _dg(w.astype(BF16), kv[:, HW:])

    acc_scr[...] = jnp.zeros_like(acc_scr)
    cs_scr[...] = jnp.zeros_like(cs_scr)
    tile(i * Q, Q, uo_ref, _iota((R, Q), 1) < _iota((R, Q), 0) % Q)

    @pl.when(i % 2 == 1)
    def _():
        tile((i - 1) * Q, Q, uo_ref, None)

    n_pairs = i // 2

    def body(jj, c):
        tile((n_pairs - 1 - jj) * 2 * Q, 2 * Q, uo2_ref, None)
        return c

    lax.fori_loop(0, n_pairs, body, 0)
    for bi in range(BB):
        o_ref[bi] = _pick_heads(acc_scr[bi], headmask, Q)


def _sb_prompt_call(sbq, sbkv_b):
    bsz, L, _ = sbq.shape
    R = SB_HEADS * ATT_BLK
    BB = _row_tile(bsz, 2)
    return pl.pallas_call(
        functools.partial(_sb_prompt_kernel, BB=BB),
        grid=(bsz // BB, L // ATT_BLK),
        in_specs=[pl.BlockSpec((BB, ATT_BLK, HW), lambda b, i: (b, i, 0)),
                  pl.BlockSpec((BB, L, 2 * HW), lambda b, i: (b, 0, 0)),
                  pl.BlockSpec((ATT_BLK, ATT_BLK + LANES), lambda b, i: (0, 0)),
                  pl.BlockSpec((2 * ATT_BLK, 2 * ATT_BLK + LANES), lambda b, i: (0, 0))],
        out_specs=pl.BlockSpec((BB, ATT_BLK, HW), lambda b, i: (b, i, 0)),
        out_shape=jax.ShapeDtypeStruct((bsz, L, HW), F32),
        scratch_shapes=[pltpu.VMEM((BB, R, HW), F32), pltpu.VMEM((BB, R, ATT_BLK), F32)],
        compiler_params=_cparams(("parallel", "arbitrary")),
        name="sb_prompt",
    )(sbq, sbkv_b, _later_and_total(), _later_and_total(2 * ATT_BLK))


def _sb_sample_kernel(pt_ref, q_ref, new_ref, *rest, L, pps):
    page_refs = rest[:pps]
    uo_ref, o_ref, acc_scr, cs_scr = rest[pps:]
    s = pl.program_id(1)
    R = SB_HEADS * L
    uo = uo_ref[...]
    q16, headmask = _head_rows(q_ref[...], L)

    @pl.when(s == 0)
    def _():
        new = new_ref[...]
        mask = _iota((R, ATT_BLK), 1) < (_iota((R, ATT_BLK), 0) % L)
        zl, btw, tot = _sb_logs(_dg(q16, new[:, :HW].astype(BF16), NT), uo, mask)
        w = jnp.where(mask, jnp.exp(zl + btw), 0.0)
        cs_scr[...] = tot
        acc_scr[...] = _dg(w.astype(BF16), new[:, HW:].astype(BF16))

    z = jnp.concatenate([_dg(q16, page_refs[r][:HW, :].astype(BF16)) for r in range(pps)], axis=0)
    zl, btw, tot = _sb_logs(z, uo, None)
    cs = cs_scr[...]
    acc = acc_scr[...]
    for r in range(pps):
        sl = slice(r * R, (r + 1) * R)
        w = jnp.exp(zl[sl] + (cs + btw[sl]))
        acc = acc + _dg(w.astype(BF16), page_refs[r][HW:, :].astype(BF16), NT)
        cs = cs + tot[sl]
    acc_scr[...] = acc
    cs_scr[...] = cs

    @pl.when(s == pl.num_programs(1) - 1)
    def _():
        o_ref[...] = _pick_heads(acc, headmask, L)


def _sb_sample_call(sbq, new_kv, cache_t, page_table, layer, pps):
    bsz, L, _ = sbq.shape
    n_pages = page_table.shape[1]
    steps = n_pages // pps
    R = SB_HEADS * L
    kern = functools.partial(_sb_sample_kernel, L=L, pps=pps)

    def page_spec(r):
        return pl.BlockSpec((None, None, 2 * HW, ATT_BLK),
                            lambda b, s, pt: (layer, pt[b, n_pages - 1 - (s * pps + r)], 0, 0))

    grid_spec = pltpu.PrefetchScalarGridSpec(
        num_scalar_prefetch=1,
        grid=(bsz, steps),
        in_specs=[pl.BlockSpec((None, L, HW), lambda b, s, pt: (b, 0, 0)),
                  pl.BlockSpec((None, ATT_BLK, 2 * HW), lambda b, s, pt: (b, 0, 0))]
                 + [page_spec(r) for r in range(pps)]
                 + [pl.BlockSpec((ATT_BLK, 2 * ATT_BLK), lambda b, s, pt: (0, 0))],
        out_specs=pl.BlockSpec((None, L, HW), lambda b, s, pt: (b, 0, 0)),
        scratch_shapes=[pltpu.VMEM((R, HW), F32), pltpu.VMEM((R, ATT_BLK), F32)],
    )
    return pl.pallas_call(
        kern,
        grid_spec=grid_spec,
        out_shape=jax.ShapeDtypeStruct((bsz, L, HW), F32),
        compiler_params=_cparams(("parallel", "arbitrary")),
        name="sb_sample",
    )(page_table, sbq, new_kv, *([cache_t] * pps), _later_and_total())


def _masked_softmax(s, valid):
    sm = jnp.where(valid, s, NEG)
    e = jnp.where(valid, jnp.exp(sm - jnp.max(sm, axis=1, keepdims=True)), 0.0)
    return e / jnp.maximum(jnp.sum(e, axis=1, keepdims=True), 1e-30)


def _top_select(score, n_sel, floor):
    lane = _iota(score.shape, 1)
    big = score.shape[1]
    sel = jnp.zeros(score.shape, F32)
    for _ in range(n_sel):
        m = jnp.max(score, axis=1, keepdims=True)
        idx = jnp.min(jnp.where(score == m, lane, big), axis=1, keepdims=True)
        hit = lane == idx
        sel = jnp.where(hit, 1.0, sel)
        score = jnp.where(hit, floor, score)
    return sel


def _nsa_prompt_kernel(q_ref, rows_ref, win_ref, misc_ref, ex_ref, o_ref,
                       kcvc_scr, kse_scr, vse_scr, kwe_scr, vwe_scr, score_scr, pick_scr, s_scr, mrun_scr,
                       m_scr, acc_scr, *, L):
    i = pl.program_id(1)
    nb = L // NSA_BLOCK
    Q = ATT_BLK
    G = NSA_HEADS
    R = G * Q

    @pl.when(i == 0)
    def _():
        blk = rows_ref[:, :2 * HEAD_DIM].reshape(nb, NSA_BLOCK, 2 * HEAD_DIM)
        kcvc_scr[...] = jnp.sum(blk, axis=1) * (1.0 / NSA_BLOCK)
        ksvs = rows_ref[:, 2 * HEAD_DIM:]
        kse_scr[...] = _low_half(ksvs).astype(BF16)
        vse_scr[...] = _low_half(_swap_halves(ksvs), 1.0).astype(BF16)
        kwvw = win_ref[...]
        kwe_scr[...] = _low_half(kwvw).astype(BF16)
        vwe_scr[...] = _low_half(_swap_halves(kwvw), 1.0).astype(BF16)

    q4 = (_heads_to_rows(q_ref[...]) * (HEAD_DIM ** -0.5)).astype(BF16)
    qpos1 = i * Q + _iota((Q, 1), 0)
    qpos4 = jnp.concatenate([qpos1] * G, axis=0)

    kcvc = kcvc_scr[...]
    s_c = _dg(q4, _low_half(kcvc).astype(BF16), NT)
    valid_c = (_iota((R, nb), 1) + 1) * NSA_BLOCK - 1 <= qpos4
    p_c = _masked_softmax(s_c, valid_c)
    o_c = _dg(p_c.astype(BF16), _swap_halves(kcvc).astype(BF16))

    imp = sum(p_c[h * Q:(h + 1) * Q] for h in range(G))
    blk1 = _iota((Q, nb), 1)
    score = jnp.where(blk1 == qpos1 // NSA_BLOCK, NSA_FORCE, jnp.where(blk1 * NSA_BLOCK <= qpos1, imp, -1.0))
    score_t = score.T
    score_scr[...] = score_t
    nidx = _iota((nb, Q), 0)

    def rank_body(m, rank):
        row = score_scr[pl.ds(m, 1), :]
        earlier = jnp.where(nidx > m, 1.0, 0.0)
        return rank + jnp.where(row > score_t, 1.0, jnp.where(row == score_t, earlier, 0.0))

    rank = lax.fori_loop(0, nb, rank_body, jnp.zeros((nb, Q), F32), unroll=8)
    sel_t = jnp.where(rank < min(NSA_TOPK, nb), 1.0, 0.0).astype(BF16)
    picked = _dg(sel_t, ex_ref[...], TN)
    causal = _iota((Q, L), 1) <= qpos1
    pick_scr[...] = jnp.where(causal & (picked > 0.5), 0.0, NEG)

    n_tiles = (i * Q + Q + SEL_TILE - 1) // SEL_TILE
    n_chunks = SEL_TILE // LANES
    mrun_scr[...] = jnp.full(mrun_scr.shape, NEG, F32)

    def pass1(j, c):
        off = pl.multiple_of(j * SEL_TILE, SEL_TILE)
        bias = pick_scr[:, pl.ds(off, SEL_TILE)]
        kb = kse_scr[pl.ds(off, SEL_TILE), :]
        for h in range(G):
            sl = slice(h * Q, (h + 1) * Q)
            sm = _dg(q4[sl], kb, NT) + bias
            s_scr[sl, pl.ds(off, SEL_TILE)] = sm
            mh = functools.reduce(jnp.maximum, [sm[:, k * LANES:(k + 1) * LANES] for k in range(n_chunks)])
            mrun_scr[sl, :] = jnp.maximum(mrun_scr[sl, :], mh)
        return c

    lax.fori_loop(0, n_tiles, pass1, 0)
    m_scr[...] = jnp.broadcast_to(jnp.max(mrun_scr[...], axis=1, keepdims=True), (R, LANES))
    acc_scr[...] = jnp.zeros_like(acc_scr)

    def pass2(j, c):
        off = pl.multiple_of(j * SEL_TILE, SEL_TILE)
        vb = vse_scr[pl.ds(off, SEL_TILE), :]
        for h in range(G):
            sl = slice(h * Q, (h + 1) * Q)
            p = jnp.exp(s_scr[sl, pl.ds(off, SEL_TILE)] - jnp.concatenate([m_scr[sl, :]] * n_chunks, axis=1))
            acc_scr[sl, :] += _dg(p.astype(BF16), vb)
        return c

    lax.fori_loop(0, n_tiles, pass2, 0)
    acc = acc_scr[...]
    o_s = acc / jnp.maximum(_swap_halves(acc), 1e-30)

    col = _iota((Q, ATT_BLK), 1)
    n_win = NSA_WINDOW // ATT_BLK + 1
    offs, biases = [], []
    for t in range(n_win):
        j = i - t
        offs.append(pl.multiple_of(jnp.maximum(j, 0) * ATT_BLK, ATT_BLK))
        kpos = j * ATT_BLK + col
        dist = qpos1 - kpos
        biases.append(jnp.where((dist >= 0) & (dist <= NSA_WINDOW) & (kpos >= 0), 0.0, NEG))
    bias_w = jnp.concatenate(biases, axis=1)
    kw = jnp.concatenate([kwe_scr[pl.ds(o, ATT_BLK), :] for o in offs], axis=0)
    vw = jnp.concatenate([vwe_scr[pl.ds(o, ATT_BLK), :] for o in offs], axis=0)
    o_w_parts = []
    for h in range(G):
        sm = _dg(q4[h * Q:(h + 1) * Q], kw, NT) + bias_w
        p = jnp.exp(sm - jnp.max(sm, axis=1, keepdims=True))
        o_w_parts.append(_dg(p.astype(BF16), vw))
    acc = jnp.concatenate(o_w_parts, axis=0)
    o_w = acc / jnp.maximum(_swap_halves(acc), 1e-30)

    gates = jax.nn.sigmoid(misc_ref[:, GLA_RANK:GLA_RANK + 3 * G])
    parts = []
    for h in range(G):
        sl = slice(h * Q, (h + 1) * Q)
        parts.append(gates[:, 3 * h:3 * h + 1] * o_c[sl] + gates[:, 3 * h + 1:3 * h + 2] * o_s[sl]
                     + gates[:, 3 * h + 2:3 * h + 3] * o_w[sl])
    o_ref[...] = _rows_to_heads(parts)


def _nsa_prompt_call(nq, rows, win, misc):
    bsz, L, _ = nq.shape
    nb = L // NSA_BLOCK
    R = NSA_HEADS * ATT_BLK
    expand = jnp.asarray(np.arange(nb)[:, None] == np.arange(L)[None, :] // NSA_BLOCK, BF16)
    kern = functools.partial(_nsa_prompt_kernel, L=L)
    return pl.pallas_call(
        kern,
        grid=(bsz, L // ATT_BLK),
        in_specs=[pl.BlockSpec((None, ATT_BLK, HW), lambda b, i: (b, i, 0)),
                  pl.BlockSpec((None, L, HW), lambda b, i: (b, 0, 0)),
                  pl.BlockSpec((None, L, 2 * HEAD_DIM), lambda b, i: (b, 0, 0)),
                  pl.BlockSpec((None, ATT_BLK, LANES), lambda b, i: (b, i, 0)),
                  pl.BlockSpec((nb, L), lambda b, i: (0, 0))],
        out_specs=pl.BlockSpec((None, ATT_BLK, HW), lambda b, i: (b, i, 0)),
        out_shape=jax.ShapeDtypeStruct((bsz, L, HW), F32),
        scratch_shapes=[pltpu.VMEM((nb, LANES), F32),
                        pltpu.VMEM((L, LANES), BF16), pltpu.VMEM((L, LANES), BF16),
                        pltpu.VMEM((L, LANES), BF16), pltpu.VMEM((L, LANES), BF16),
                        pltpu.VMEM((nb, ATT_BLK), F32), pltpu.VMEM((ATT_BLK, L), F32),
                        pltpu.VMEM((R, L), F32), pltpu.VMEM((R, LANES), F32),
                        pltpu.VMEM((R, LANES), F32), pltpu.VMEM((R, LANES), F32)],
        compiler_params=_cparams(("parallel", "arbitrary")),
        name="nsa_prompt",
    )(nq, rows, win, misc, expand)


def _nsa_sample_kernel(pt_ref, q_ref, new_ref, winp_ref, winn_ref, misc_ref, ea_ref, eb_ref, *rest,
                       L, pps, n_pages):
    page_refs = rest[:pps]
    o_ref, kcvc_scr, ksvs_scr = rest[pps:]
    s = pl.program_id(1)
    steps = n_pages // pps
    G = NSA_HEADS
    R = G * L
    per_page = ATT_BLK // NSA_BLOCK
    bps = pps * per_page
    gk = pps * ATT_BLK
    nbp = n_pages * per_page
    nbp_pad = kcvc_scr.shape[1]
    q4 = (_heads_to_rows(q_ref[...]) * (HEAD_DIM ** -0.5)).astype(BF16)
    t_row = _iota((R, 1), 0) % L

    @pl.when(s == 0)
    def _():
        kcvc_scr[...] = jnp.zeros_like(kcvc_scr)

    x = jnp.concatenate([page_refs[r][:2 * HEAD_DIM, :] for r in range(pps)], axis=1)
    means = _dot_xl(x, ea_ref[...])
    base = (s * bps) % LANES
    off = pl.multiple_of(((s * bps) // LANES) * LANES, LANES)
    kcvc_scr[:, pl.ds(off, LANES)] += pltpu.roll(means, base, 1)
    for r in range(pps):
        koff = pl.multiple_of((s * pps + r) * ATT_BLK, ATT_BLK)
        ksvs_scr[:, pl.ds(koff, ATT_BLK)] = page_refs[r][2 * HEAD_DIM:, :].astype(BF16)

    @pl.when(s == steps - 1)
    def _():
        kcvc = kcvc_scr[...].astype(BF16)
        live = _iota((R, nbp_pad), 1) < nbp
        s_c = _dg(q4, kcvc)
        p_c = _masked_softmax(s_c, live)
        o_c = _swap_halves(_dg(p_c.astype(BF16), kcvc, NT))
        imp = sum(p_c[h * L:(h + 1) * L] for h in range(G))
        imp = jnp.where(live[:L], imp, -3.0)
        sel = _top_select(imp, min(NSA_TOPK, nbp + 1) - 1, -2.0)
        sel = jnp.concatenate([sel] * G, axis=0).astype(BF16)
        n_buf = winp_ref.shape[1]
        winp = winp_ref[...].astype(BF16)
        winn = winn_ref[...]
        dist_p = t_row + n_buf - _iota((R, n_buf), 1)
        dist_n = t_row - _iota((R, ATT_BLK), 1)
        s_w = jnp.concatenate([_dg(q4, winp), _dg(q4, _low_half(winn).astype(BF16), NT)], axis=1)
        valid_w = jnp.concatenate([(dist_p >= 0) & (dist_p <= NSA_WINDOW),
                                   (dist_n >= 0) & (dist_n <= NSA_WINDOW)], axis=1)
        p_w = _masked_softmax(s_w, valid_w).astype(BF16)
        o_w = _swap_halves(_dg(p_w[:, :n_buf], winp, NT) + _dg(p_w[:, n_buf:], winn.astype(BF16)))
        ksvs_new = new_ref[:, 2 * HEAD_DIM:]
        sms = [jnp.where(dist_n >= 0, _dg(q4, _low_half(ksvs_new).astype(BF16), NT), NEG)]
        for g in range(steps):
            picked = _dg(sel[:, g * bps:(g + 1) * bps], eb_ref[...]) > 0.5
            sms.append(jnp.where(picked, _dg(q4, ksvs_scr[:, g * gk:(g + 1) * gk]), NEG))
        m = functools.reduce(jnp.maximum, [jnp.max(t, axis=1, keepdims=True) for t in sms])
        p0 = jnp.exp(sms[0] - m)
        l = jnp.sum(p0, axis=1, keepdims=True)
        acc = _dg(p0.astype(BF16), ksvs_new.astype(BF16))
        for g in range(steps):
            p = jnp.exp(sms[g + 1] - m)
            l = l + jnp.sum(p, axis=1, keepdims=True)
            acc = acc + _dg(p.astype(BF16), ksvs_scr[:, g * gk:(g + 1) * gk], NT)
        o_s = _swap_halves(acc) / jnp.maximum(l, 1e-30)
        gates = jax.nn.sigmoid(misc_ref[:, GLA_RANK:GLA_RANK + 3 * G])
        parts = []
        for h in range(G):
            sl = slice(h * L, (h + 1) * L)
            parts.append(gates[:, 3 * h:3 * h + 1] * o_c[sl] + gates[:, 3 * h + 1:3 * h + 2] * o_s[sl]
                         + gates[:, 3 * h + 2:3 * h + 3] * o_w[sl])
        o_ref[...] = _rows_to_heads(parts)


def _nsa_sample_call(nq, new_rows, win_past_t, win_new, misc, cache_t, page_table, layer, pps):
    bsz, L, _ = nq.shape
    n_pages = page_table.shape[1]
    steps = n_pages // pps
    n_buf = win_past_t.shape[3]
    per_page = ATT_BLK // NSA_BLOCK
    bps = pps * per_page
    nbp = n_pages * per_page
    nbp_pad = -(-nbp // LANES) * LANES
    assert LANES % bps == 0
    R = NSA_HEADS * L
    keys = np.arange(pps * ATT_BLK)
    ea = jnp.asarray((keys[:, None] // NSA_BLOCK == np.arange(LANES)[None, :]) / float(NSA_BLOCK), BF16)
    eb = jnp.asarray(np.arange(bps)[:, None] == keys[None, :] // NSA_BLOCK, BF16)
    kern = functools.partial(_nsa_sample_kernel, L=L, pps=pps, n_pages=n_pages)

    def page_spec(r):
        return pl.BlockSpec((None, None, HW, ATT_BLK), lambda b, s, pt: (layer, pt[b, s * pps + r], 0, 0))

    cst = lambda shape: pl.BlockSpec(shape, lambda b, s, pt: (0,) * len(shape))
    grid_spec = pltpu.PrefetchScalarGridSpec(
        num_scalar_prefetch=1,
        grid=(bsz, steps),
        in_specs=[pl.BlockSpec((None, L, HW), lambda b, s, pt: (b, 0, 0)),
                  pl.BlockSpec((None, ATT_BLK, HW), lambda b, s, pt: (b, 0, 0)),
                  pl.BlockSpec((None, None, 2 * HEAD_DIM, n_buf), lambda b, s, pt: (layer, b, 0, 0)),
                  pl.BlockSpec((None, ATT_BLK, 2 * HEAD_DIM), lambda b, s, pt: (b, 0, 0)),
                  pl.BlockSpec((None, L, LANES), lambda b, s, pt: (b, 0, 0)),
                  cst((pps * ATT_BLK, LANES)), cst((bps, pps * ATT_BLK))]
                 + [page_spec(r) for r in range(pps)],
        out_specs=pl.BlockSpec((None, L, HW), lambda b, s, pt: (b, 0, 0)),
        scratch_shapes=[pltpu.VMEM((2 * HEAD_DIM, nbp_pad), F32),
                        pltpu.VMEM((2 * HEAD_DIM, n_pages * ATT_BLK), BF16)],
    )
    return pl.pallas_call(
        kern,
        grid_spec=grid_spec,
        out_shape=jax.ShapeDtypeStruct((bsz, L, HW), F32),
        compiler_params=_cparams(("parallel", "arbitrary")),
        name="nsa_sample",
    )(page_table, nq, new_rows, win_past_t, win_new, misc, ea, eb, *([cache_t] * pps))


def _outproj_kernel(ya_ref, yb_ref, yc_ref, yd_ref, x_ref, w_ref, g_ref, b_ref, o_ref):
    y = jnp.concatenate([ya_ref[...], yb_ref[...], yc_ref[...], yd_ref[...]], axis=1).astype(BF16)
    h = jnp.dot(y, w_ref[...], preferred_element_type=F32)
    o_ref[...] = _layer_norm(DN_ALPHA * x_ref[...] + h, g_ref[...], b_ref[...])


def _outproj_call(ya, yb, yc, yd, x, w, g, b, tm):
    n = x.shape[0]
    part = lambda: pl.BlockSpec((tm, HW), lambda i: (i, 0))
    return pl.pallas_call(
        _outproj_kernel,
        grid=(n // tm,),
        in_specs=[part(), part(), part(), part(),
                  pl.BlockSpec((tm, D_MODEL), lambda i: (i, 0)),
                  pl.BlockSpec((4 * HW, D_MODEL), lambda i: (0, 0)),
                  pl.BlockSpec((1, D_MODEL), lambda i: (0, 0)),
                  pl.BlockSpec((1, D_MODEL), lambda i: (0, 0))],
        out_specs=pl.BlockSpec((tm, D_MODEL), lambda i: (i, 0)),
        out_shape=jax.ShapeDtypeStruct((n, D_MODEL), F32),
        compiler_params=_cparams(("parallel",)),
        name="out_proj_ln",
    )(ya, yb, yc, yd, x, w, g.reshape(1, D_MODEL), b.reshape(1, D_MODEL))


def _route(logits):
    lane = _iota(logits.shape, 1)
    live = lane < N_EXPERTS
    lg = jnp.where(live, logits, NEG)
    e = jnp.where(live, jnp.exp(lg - jnp.max(lg, axis=1, keepdims=True)), 0.0)
    probs = e / jnp.sum(e, axis=1, keepdims=True)
    big = logits.shape[1]

    def top2(vals):
        w1 = jnp.max(vals, axis=1, keepdims=True)
        i1 = jnp.min(jnp.where(vals == w1, lane, big), axis=1, keepdims=True)
        rest = jnp.where(lane == i1, -2.0, vals)
        w2 = jnp.max(rest, axis=1, keepdims=True)
        i2 = jnp.min(jnp.where(rest == w2, lane, big), axis=1, keepdims=True)
        return w1, i1, w2, i2

    best = None
    g_sel = None
    for g in range(N_GROUPS):
        in_g = (lane // EXPERTS_PER_GROUP) == g
        w1, _, w2, _ = top2(jnp.where(in_g, probs, -1.0))
        tot = w1 + w2
        if best is None:
            best, g_sel = tot, jnp.zeros_like(tot, dtype=jnp.int32)
        else:
            upd = tot > best
            g_sel = jnp.where(upd, g, g_sel)
            best = jnp.where(upd, tot, best)
    in_grp = live & ((lane // EXPERTS_PER_GROUP) == g_sel)
    w1, i1, w2, i2 = top2(jnp.where(in_grp, probs, -1.0))
    den = w1 + w2
    return jnp.where(lane == i1, w1 / den, 0.0) + jnp.where(lane == i2, w2 / den, 0.0)


def _moe_kernel(x_ref, rw_ref, wg_ref, wu_ref, wd_ref, g_ref, b_ref, o_ref, acc_scr, comb_scr, xb_scr):
    e = pl.program_id(1)

    @pl.when(e == 0)
    def _():
        x = x_ref[...]
        xb_scr[...] = x.astype(BF16)
        comb_scr[...] = _route(_dot3(x, rw_ref[...]))
        acc_scr[...] = jnp.zeros_like(acc_scr)

    xb = xb_scr[...]
    hg = jnp.dot(xb, wg_ref[...], preferred_element_type=F32)
    hu = jnp.dot(xb, wu_ref[...], preferred_element_type=F32)
    comb = comb_scr[...]
    c = jnp.sum(jnp.where(_iota(comb.shape, 1) == e, comb, 0.0), axis=1, keepdims=True)
    h = (hg * jax.nn.sigmoid(hg)) * hu * c
    acc_scr[...] += jnp.dot(h.astype(BF16), wd_ref[...], preferred_element_type=F32)

    @pl.when(e == pl.num_programs(1) - 1)
    def _():
        o_ref[...] = _layer_norm(DN_ALPHA * x_ref[...] + acc_scr[...], g_ref[...], b_ref[...])


def _moe_call(x, rw, wg, wu, wd, g, b, tm):
    n = x.shape[0]
    return pl.pallas_call(
        _moe_kernel,
        grid=(n // tm, N_EXPERTS),
        in_specs=[pl.BlockSpec((tm, D_MODEL), lambda i, e: (i, 0)),
                  pl.BlockSpec((D_MODEL, LANES), lambda i, e: (0, 0)),
                  pl.BlockSpec((None, D_MODEL, D_EXPERT), lambda i, e: (e, 0, 0)),
                  pl.BlockSpec((None, D_MODEL, D_EXPERT), lambda i, e: (e, 0, 0)),
                  pl.BlockSpec((None, D_EXPERT, D_MODEL), lambda i, e: (e, 0, 0)),
                  pl.BlockSpec((1, D_MODEL), lambda i, e: (0, 0)),
                  pl.BlockSpec((1, D_MODEL), lambda i, e: (0, 0))],
        out_specs=pl.BlockSpec((tm, D_MODEL), lambda i, e: (i, 0)),
        out_shape=jax.ShapeDtypeStruct((n, D_MODEL), F32),
        scratch_shapes=[pltpu.VMEM((tm, D_MODEL), F32), pltpu.VMEM((tm, LANES), F32),
                        pltpu.VMEM((tm, D_MODEL), BF16)],
        compiler_params=_cparams(("parallel", "arbitrary")),
        name="moe_ln",
    )(x, rw, wg, wu, wd, g.reshape(1, D_MODEL), b.reshape(1, D_MODEL))


def _pad_rows(t, n):
    return jnp.pad(t, ((0, 0), (0, n - t.shape[1]), (0, 0)))


def _row_tile(n, pref):
    t = min(pref, n)
    while n % t:
        t //= 2
    return t


def _keys_minor(cache):
    d, p, rows = cache.shape[:3]
    return jnp.transpose(cache, (0, 1, 3, 4, 5, 2)).reshape(d, p, -1, rows)


def kernel(x_prompt, x_sample, cache_sb_kv, cache_nsa_kv, cache_nsa_win, state_gla, state_lru_h, state_lru_conv, page_table, ln_in_g, ln_in_b, w_in, gla_w_a2, gla_b_a, gla_norm_g, lru_conv_w, lru_conv_b, lru_w_a, lru_b_a, lru_w_x, lru_b_x, lru_lambda, w_out, ln1_g, ln1_b, router_w, moe_w_gate, moe_w_up, moe_w_down, ln2_g, ln2_b):
    bp, lp, d = x_prompt.shape
    bs, ls, _ = x_sample.shape
    depth = w_in.shape[0]
    page = cache_sb_kv.shape[2]
    n_pages = page_table.shape[1]
    past_len = n_pages * page
    n_buf = cache_nsa_win.shape[2]
    n_p, n_s = bp * lp, bs * ls
    assert page == ATT_BLK and lp % SEL_TILE == 0 and ls <= 8 and d == D_MODEL
    assert n_pages * (ATT_BLK // NSA_BLOCK) >= NSA_TOPK - 1 and n_buf <= past_len

    tm_p = _row_tile(n_p, 512)
    tm_s = _row_tile(n_s, 512)
    tab_p = _rope_table(jnp.arange(lp))
    tab_s = jnp.tile(_rope_table(past_len + jnp.arange(ls)), (bs, 1))
    assert lp % tm_p == 0 and tm_s == n_s

    cache_sb_t = _keys_minor(cache_sb_kv)
    cache_nsa_t = _keys_minor(cache_nsa_kv)
    win_past_t = _keys_minor(cache_nsa_win)
    win_past = cache_nsa_win.reshape(depth, bs, n_buf, 2 * HEAD_DIM)
    rw = jnp.pad(router_w, ((0, 0), (0, LANES - N_EXPERTS)))
    pps = 16
    while n_pages % pps:
        pps //= 2
    gla_c = 32
    lru_t = _row_tile(lp, 512)

    xp = _ln_call(x_prompt.reshape(n_p, d), ln_in_g, ln_in_b, tm_p)
    xs = _ln_call(x_sample.reshape(n_s, d), ln_in_g, ln_in_b, tm_s)

    st_p, st_s = [], []
    for l in range(depth):
        w_l = _prep_w_in(w_in[l])
        lru_w = _lru_weights(lru_conv_w[l], lru_conv_b[l], lru_w_a[l], lru_b_a[l], lru_w_x[l], lru_b_x[l],
                             lru_lambda[l])
        w_out_l = w_out[l].astype(BF16)
        wg, wu, wd = moe_w_gate[l].astype(BF16), moe_w_up[l].astype(BF16), moe_w_down[l].astype(BF16)

        gla, lru, sbq, sbkv, nq, nsa, win, misc, sbkv_b = _proj_call(xp, w_l, tab_p, tm_p)
        r3 = lambda t, b_, l_: t.reshape(b_, l_, t.shape[-1])
        gla3, lru3, sbq3, sbkv3 = r3(gla, bp, lp), r3(lru, bp, lp), r3(sbq, bp, lp), r3(sbkv, bp, lp)
        nq3, nsa3, win3, misc3 = r3(nq, bp, lp), r3(nsa, bp, lp), r3(win, bp, lp), r3(misc, bp, lp)
        y_a, gla_t = _gla_call(gla3, misc3, gla_w_a2[l], gla_b_a[l], gla_norm_g[l],
                               jnp.zeros((bp, GV_W, QK_W), F32), gla_c, gla_c)
        y_b, lru_h = _lru_prompt_call(lru3, lru_w, lru_t)
        y_c = _sb_prompt_call(sbq3, sbkv_b.reshape(bp, lp, 2 * HW))
        y_d = _nsa_prompt_call(nq3, nsa3, win3, misc3)
        x1 = _outproj_call(y_a.reshape(n_p, HW), y_b.reshape(n_p, HW), y_c.reshape(n_p, HW), y_d.reshape(n_p, HW),
                           xp, w_out_l, ln1_g[l], ln1_b[l], tm_p)
        xp = _moe_call(x1, rw, wg, wu, wd, ln2_g[l], ln2_b[l], _row_tile(n_p, 1024))
        wn = min(NSA_WINDOW, lp)
        st_p.append((sbkv3.reshape(bp, lp, 2, SB_HEADS, HEAD_DIM), nsa3.reshape(bp, lp, 4, 1, HEAD_DIM),
                     win3[:, lp - wn:].reshape(bp, wn, 2, 1, HEAD_DIM), _gla_state_from_t(gla_t),
                     lru_h.reshape(bp, LRU_WIDTH), lru3[:, lp - (CONV_W - 1):, :LRU_WIDTH]))

        gla, lru, sbq, sbkv, nq, nsa, win, misc, _ = _proj_call(xs, w_l, tab_s, tm_s)
        gla3, lru3, sbq3, sbkv3 = r3(gla, bs, ls), r3(lru, bs, ls), r3(sbq, bs, ls), r3(sbkv, bs, ls)
        nq3, nsa3, win3, misc3 = r3(nq, bs, ls), r3(nsa, bs, ls), r3(win, bs, ls), r3(misc, bs, ls)
        gc = 16
        y_a, gla_t = _gla_call(_pad_rows(gla3, gc), _pad_rows(misc3, gc), gla_w_a2[l], gla_b_a[l], gla_norm_g[l],
                               _gla_state_to_t(state_gla[l].astype(F32)), gc, ls)
        y_a = y_a[:, :ls]
        tmaj = lambda t: jnp.swapaxes(t, 0, 1)
        y_b, lru_h = _lru_sample_call(tmaj(lru3[:, :, :LRU_WIDTH]), tmaj(lru3[:, :, LRU_WIDTH:]),
                                      tmaj(state_lru_conv[l]), state_lru_h[l].astype(F32), lru_w)
        y_b = tmaj(y_b)
        y_c = _sb_sample_call(sbq3, _pad_rows(sbkv3, ATT_BLK), cache_sb_t, page_table, l, pps)
        y_d = _nsa_sample_call(nq3, _pad_rows(nsa3, ATT_BLK), win_past_t, _pad_rows(win3, ATT_BLK), misc3,
                               cache_nsa_t, page_table, l, pps)
        x1 = _outproj_call(y_a.reshape(n_s, HW), y_b.reshape(n_s, HW), y_c.reshape(n_s, HW), y_d.reshape(n_s, HW),
                           xs, w_out_l, ln1_g[l], ln1_b[l], tm_s)
        xs = _moe_call(x1, rw, wg, wu, wd, ln2_g[l], ln2_b[l], tm_s)
        win_all = jnp.concatenate([win_past[l], win3], axis=1)
        lru_buf = jnp.concatenate([state_lru_conv[l], lru3[:, :, :LRU_WIDTH]], axis=1)[:, ls:]
        st_s.append((sbkv3.reshape(bs, ls, 2, SB_HEADS, HEAD_DIM), nsa3.reshape(bs, ls, 4, 1, HEAD_DIM),
                     win_all[:, win_all.shape[1] - n_buf:].reshape(bs, n_buf, 2, 1, HEAD_DIM),
                     _gla_state_from_t(gla_t), lru_h, lru_buf))

    outs_p = [jnp.stack([st[i] for st in st_p]) for i in range(6)]
    outs_s = [jnp.stack([st[i] for st in st_s]) for i in range(6)]
    res = [xp.reshape(bp, lp, d), xs.reshape(bs, ls, d)]
    for a, b in zip(outs_p, outs_s):
        res += [a, b]
    return tuple(res)
```

```python
import functools

import jax
import jax.numpy as jnp
import numpy as np
from jax import lax
from jax.experimental import pallas as pl
from jax.experimental.pallas import tpu as pltpu

F32 = jnp.float32
BF16 = jnp.bfloat16

D_MODEL = 1024
HEAD_DIM = 64
GLA_HEADS, GLA_DK, GLA_DV, GLA_RANK, GLA_TAU = 4, 32, 64, 16, 16.0
LRU_WIDTH, LRU_HEADS, CONV_W, LRU_C = 256, 4, 4, 8.0
LRU_BLOCK = LRU_WIDTH // LRU_HEADS
SB_HEADS = 4
NSA_HEADS, NSA_BLOCK, NSA_TOPK, NSA_WINDOW, NSA_FORCE = 4, 64, 16, 512, 1.0e4
ROPE_THETA, ROPE_DIM = 500000.0, HEAD_DIM // 4
N_EXPERTS, N_GROUPS, TOP_K, D_EXPERT = 16, 4, 2, 512
EXPERTS_PER_GROUP = N_EXPERTS // N_GROUPS
DEPTH = 2
DN_ALPHA = (2.0 * DEPTH) ** 0.25
EPS = 1e-5
NEG = -1e30

QK_W = GLA_HEADS * GLA_DK
GV_W = GLA_HEADS * GLA_DV
HW = 4 * HEAD_DIM
IN_SIZES = (QK_W, QK_W, GV_W, GV_W, GLA_RANK, LRU_WIDTH, LRU_WIDTH, HW, HW, HW, HW) + (HEAD_DIM,) * 6 + (12,)
IN_OFFSETS = tuple(int(v) for v in np.cumsum(IN_SIZES)[:-1])
P_GLA, P_LRU, P_SBQ, P_SBKV, P_NQ, P_NSA, P_WIN, P_MISC = 0, 768, 1280, 1536, 2048, 2304, 2560, 2688
P_TOTAL = 2816
LANES = 128
ATT_BLK = 128
SEL_TILE = 4 * ATT_BLK
VMEM_LIMIT = 56 * 1024 * 1024
SB_DEAD = 105.0

NN = (((1,), (0,)), ((), ()))
NT = (((1,), (1,)), ((), ()))
TN = (((0,), (0,)), ((), ()))


def _cparams(sem):
    return pltpu.CompilerParams(dimension_semantics=sem, vmem_limit_bytes=VMEM_LIMIT)


def _split2(a):
    hi = a.astype(BF16)
    lo = (a - hi.astype(F32)).astype(BF16)
    return hi, lo


def _split3(a):
    hi = a.astype(BF16)
    r = a - hi.astype(F32)
    mid = r.astype(BF16)
    lo = (r - mid.astype(F32)).astype(BF16)
    return hi, mid, lo


def _dg(a, b, dims=NN):
    return lax.dot_general(a, b, dims, preferred_element_type=F32)


def _dot1(a, b, dims=NN):
    return _dg(a.astype(BF16), b.astype(BF16), dims)


def _dot3(a, b, dims=NN):
    ah, al = _split2(a)
    bh, bl = _split2(b)
    return _dg(ah, bh, dims) + (_dg(ah, bl, dims) + _dg(al, bh, dims))


def _dot_xl(a, b_exact, dims=NN):
    h, m, l = _split3(a)
    return _dg(h, b_exact, dims) + (_dg(m, b_exact, dims) + _dg(l, b_exact, dims))


def _dot_lx(a_exact, b, dims=NN):
    h, m, l = _split3(b)
    return _dg(a_exact, h, dims) + (_dg(a_exact, m, dims) + _dg(a_exact, l, dims))


def _softplus(x):
    return jnp.maximum(x, 0.0) + jnp.log(1.0 + jnp.exp(-jnp.abs(x)))


def _log_sigmoid(x):
    return -_softplus(-x)


def _layer_norm(x, g, b):
    mu = jnp.mean(x, axis=-1, keepdims=True)
    xc = x - mu
    var = jnp.mean(xc * xc, axis=-1, keepdims=True)
    return xc * lax.rsqrt(var + EPS) * g + b


def _iota(shape, dim):
    return lax.broadcasted_iota(jnp.int32, shape, dim)


def _low_half(x, fill=0.0):
    return jnp.where(_iota(x.shape, 1) < HEAD_DIM, x, fill)


def _swap_halves(x):
    return pltpu.roll(x, HEAD_DIM, 1)


def _heads_to_rows(q):
    pieces = []
    for h in range(HW // HEAD_DIM):
        pair = q[:, (h // 2) * LANES:(h // 2 + 1) * LANES]
        pieces.append(_low_half(pair if h % 2 == 0 else _swap_halves(pair)))
    return jnp.concatenate(pieces, axis=0)


def _rows_to_heads(parts):
    pairs = [jnp.where(_iota(parts[0].shape, 1) < HEAD_DIM, parts[2 * p], _swap_halves(parts[2 * p + 1]))
             for p in range(len(parts) // 2)]
    return jnp.concatenate(pairs, axis=1)


def _ln_kernel(x_ref, g_ref, b_ref, o_ref):
    o_ref[...] = _layer_norm(x_ref[...], g_ref[...], b_ref[...])


def _ln_call(x, g, b, tm):
    n, d = x.shape
    return pl.pallas_call(
        _ln_kernel,
        grid=(n // tm,),
        in_specs=[pl.BlockSpec((tm, d), lambda i: (i, 0)),
                  pl.BlockSpec((1, d), lambda i: (0, 0)),
                  pl.BlockSpec((1, d), lambda i: (0, 0))],
        out_specs=pl.BlockSpec((tm, d), lambda i: (i, 0)),
        out_shape=jax.ShapeDtypeStruct((n, d), F32),
        compiler_params=_cparams(("parallel",)),
        name="ln_in",
    )(x, g.reshape(1, d), b.reshape(1, d))


def _rope(v, c, s, flags):
    ones = jnp.ones_like(c)
    zeros = jnp.zeros_like(s)
    cc = jnp.concatenate([c if f else ones for f in flags], axis=-1)
    ss = jnp.concatenate([s if f else zeros for f in flags], axis=-1)
    n = v.shape[-1]
    half = ROPE_DIM // 2
    lane = _iota(v.shape, 1) % HEAD_DIM
    sw = jnp.where(lane < half, pltpu.roll(v, n - half, 1), pltpu.roll(v, half, 1))
    return v * cc + sw * ss


def _proj_kernel(x_ref, w_ref, tab_ref, gla_ref, lru_ref, sbq_ref, sbkv_ref, nq_ref, nsa_ref, win_ref, misc_ref,
                 sbkvb_ref):
    x = x_ref[...].astype(BF16)

    def mm(off, width):
        return jnp.dot(x, w_ref[:, off:off + width], preferred_element_type=F32)

    gla_ref[...] = mm(P_GLA, 768)
    lru_ref[...] = mm(P_LRU, 512)
    sbq_ref[...] = mm(P_SBQ, 256)
    sbkv = mm(P_SBKV, 512)
    sbkv_ref[...] = sbkv
    sbkvb_ref[...] = sbkv.astype(BF16)
    tab = tab_ref[...]
    c = tab[:, :HEAD_DIM]
    s = tab[:, HEAD_DIM:]
    nq_ref[...] = _rope(mm(P_NQ, 256), c, s, (1, 1, 1, 1))
    nsa_ref[...] = _rope(mm(P_NSA, 256), c, s, (1, 0, 1, 0))
    win_ref[...] = _rope(mm(P_WIN, 128), c, s, (1, 0))
    misc_ref[...] = mm(P_MISC, 128)


def _proj_call(x, w, tab, tm):
    n = x.shape[0]
    tab_blocks = tab.shape[0] // tm
    widths = (768, 512, 256, 512, 256, 256, 128, 128)
    return pl.pallas_call(
        _proj_kernel,
        grid=(n // tm,),
        in_specs=[pl.BlockSpec((tm, D_MODEL), lambda i: (i, 0)),
                  pl.BlockSpec((D_MODEL, P_TOTAL), lambda i: (0, 0)),
                  pl.BlockSpec((tm, LANES), lambda i: (i % tab_blocks, 0))],
        out_specs=[pl.BlockSpec((tm, wd), lambda i: (i, 0)) for wd in widths + (512,)],
        out_shape=[jax.ShapeDtypeStruct((n, wd), F32) for wd in widths]
                  + [jax.ShapeDtypeStruct((n, 512), BF16)],
        compiler_params=_cparams(("parallel",)),
        name="in_proj",
    )(x, w, tab)


def _rope_table(pos):
    inv = ROPE_THETA ** (-jnp.arange(0, ROPE_DIM, 2, dtype=F32) / ROPE_DIM)
    ang = pos.astype(F32)[:, None] * inv[None, :]
    cos, sin = jnp.cos(ang), jnp.sin(ang)
    n = pos.shape[0]
    c = jnp.concatenate([cos, cos, jnp.ones((n, HEAD_DIM - ROPE_DIM), F32)], axis=-1)
    s = jnp.concatenate([-sin, sin, jnp.zeros((n, HEAD_DIM - ROPE_DIM), F32)], axis=-1)
    return jnp.concatenate([c, s], axis=-1)


def _prep_w_in(w):
    parts = jnp.split(w, IN_OFFSETS, axis=-1)
    (gq, gk, gv, gg, ga, lx, lg, sq, sk, sv, nq, nkc, nvc, nks, nvs, nkw, nvw, ngate) = parts
    pad = jnp.zeros((w.shape[0], LANES - GLA_RANK - 12), w.dtype)
    return jnp.concatenate([gq, gk, gv, gg, lx, lg, sq, sk, sv, nq, nkc, nvc, nks, nvs, nkw, nvw,
                            ga, ngate, pad], axis=-1).astype(BF16)


def _gla_kernel(q_ref, k_ref, v_ref, g_ref, misc_ref, wa2_ref, ba_ref, ng_ref, indv_ref, bd_ref, ones_ref,
                s0_ref, y_ref, st_ref, st_scr, b_scr, phi_scr, plo_scr, att_scr, *, C, Cv, BB):
    c = pl.program_id(1)

    @pl.when(c == 0)
    def _():
        st_scr[...] = s0_ref[...]

    rows = _iota((C, QK_W), 0)
    tril = (_iota((C, C), 0) >= _iota((C, C), 1)).astype(BF16)
    qs, bs = [], []
    for bi in range(BB):
        la = _log_sigmoid(_dot3(misc_ref[bi, :, :GLA_RANK], wa2_ref[...]) + ba_ref[...]) * (1.0 / GLA_TAU)
        if Cv < C:
            la = jnp.where(rows < Cv, la, 0.0)
        b = _dot_lx(tril, la)
        b_scr[bi] = b
        bs.append(b)
        qs.append(q_ref[bi] * (GLA_DK ** -0.5))

    def fill(s, carry):
        off = pl.multiple_of(s * C, C)
        for bi in range(BB):
            b_s = b_scr[bi, pl.ds(s, 1), :]
            k_s = k_ref[bi, pl.ds(s, 1), :]
            d = jnp.where(rows >= s, bs[bi] - b_s, -jnp.inf)
            hi, lo = _split2(qs[bi] * k_s * jnp.exp(d))
            phi_scr[bi, pl.ds(off, C), :] = hi
            plo_scr[bi, pl.ds(off, C), :] = lo
        return carry

    lax.fori_loop(0, Cv, fill, 0, unroll=2)
    indv = indv_ref[...]
    for bi in range(BB):
        att_scr[bi, :Cv * C, :] = _dg(phi_scr[bi, :Cv * C, :], indv) + _dg(plo_scr[bi, :Cv * C, :], indv)

    def gather(s, os):
        off = pl.multiple_of(s * C, C)
        return tuple(os[bi] + att_scr[bi, pl.ds(off, C), :] * v_ref[bi, pl.ds(s, 1), :] for bi in range(BB))

    os = lax.fori_loop(0, Cv, gather, tuple(jnp.zeros((C, GV_W), F32) for _ in range(BB)), unroll=2)
    for bi in range(BB):
        b = bs[bi]
        st = st_scr[bi]
        o = os[bi] + _dot3(qs[bi] * jnp.exp(b), st, NT)
        bl = b[C - 1:C, :]
        kd = k_ref[bi] * jnp.exp(bl - b)
        upd = _dot3(v_ref[bi], kd, TN) * bd_ref[...]
        st_new = st * jnp.exp(bl) + upd
        st_scr[bi] = st_new
        st_ref[bi] = st_new
        ms = _dot_xl(o * o, ones_ref[...])
        on = o * lax.rsqrt(ms + EPS) * ng_ref[...]
        g = g_ref[bi]
        y_ref[bi] = on * (g * jax.nn.sigmoid(g))


def _gla_call(gla, misc, wa2, ba, ng, s0_t, C, Cv):
    bsz, L, _ = gla.shape
    nc = L // C
    indv = (np.arange(QK_W)[:, None] // GLA_DK == np.arange(GV_W)[None, :] // GLA_DV)
    bd = (np.arange(GV_W)[:, None] // GLA_DV == np.arange(QK_W)[None, :] // GLA_DK)
    ones = (np.arange(GV_W)[:, None] // GLA_DV == np.arange(GV_W)[None, :] // GLA_DV) / float(GLA_DV)
    const = lambda shape: pl.BlockSpec(shape, lambda b, c: (0,) * len(shape))
    BB = _row_tile(bsz, 4)
    kern = functools.partial(_gla_kernel, C=C, Cv=Cv, BB=BB)
    return pl.pallas_call(
        kern,
        grid=(bsz // BB, nc),
        in_specs=[pl.BlockSpec((BB, C, QK_W), lambda b, c: (b, c, 0)),
                  pl.BlockSpec((BB, C, QK_W), lambda b, c: (b, c, 1)),
                  pl.BlockSpec((BB, C, GV_W), lambda b, c: (b, c, 1)),
                  pl.BlockSpec((BB, C, GV_W), lambda b, c: (b, c, 2)),
                  pl.BlockSpec((BB, C, LANES), lambda b, c: (b, c, 0)),
                  const((GLA_RANK, QK_W)), const((1, QK_W)), const((1, GV_W)),
                  const((QK_W, GV_W)), const((GV_W, QK_W)), const((GV_W, GV_W)),
                  pl.BlockSpec((BB, GV_W, QK_W), lambda b, c: (b, 0, 0))],
        out_specs=[pl.BlockSpec((BB, C, GV_W), lambda b, c: (b, c, 0)),
                   pl.BlockSpec((BB, GV_W, QK_W), lambda b, c: (b, 0, 0))],
        out_shape=[jax.ShapeDtypeStruct((bsz, L, GV_W), F32),
                   jax.ShapeDtypeStruct((bsz, GV_W, QK_W), F32)],
        scratch_shapes=[pltpu.VMEM((BB, GV_W, QK_W), F32), pltpu.VMEM((BB, C, QK_W), F32),
                        pltpu.VMEM((BB, C * C, QK_W), BF16), pltpu.VMEM((BB, C * C, QK_W), BF16),
                        pltpu.VMEM((BB, C * C, GV_W), F32)],
        compiler_params=_cparams(("parallel", "arbitrary")),
        name="gla",
    )(gla, gla, gla, gla, misc, wa2, ba.reshape(1, QK_W), ng.reshape(1, GV_W),
      jnp.asarray(indv, BF16), jnp.asarray(bd, F32), jnp.asarray(ones, BF16), s0_t)


def _gla_state_to_t(s0):
    bsz = s0.shape[0]
    st = jnp.swapaxes(s0, 2, 3)
    eye = jnp.eye(GLA_HEADS, dtype=s0.dtype)
    full = st[:, :, :, None, :] * eye[None, :, None, :, None]
    return full.reshape(bsz, GV_W, QK_W)


def _gla_state_from_t(st):
    bsz = st.shape[0]
    full = st.reshape(bsz, GLA_HEADS, GLA_DV, GLA_HEADS, GLA_DK)
    diag = jnp.stack([full[:, h, :, h, :] for h in range(GLA_HEADS)], axis=1)
    return jnp.swapaxes(diag, 2, 3)


def _lru_gates(xc, wa_ref, ba_ref, wx_ref, bx_ref, sp_ref):
    r = jax.nn.sigmoid(_dot3(xc, wa_ref[...]) + ba_ref[...])
    i = jax.nn.sigmoid(_dot3(xc, wx_ref[...]) + bx_ref[...])
    log_a = -LRU_C * r * sp_ref[...]
    a = jnp.exp(log_a)
    u = jnp.sqrt(-jnp.tanh(log_a) * (a * a + 1.0)) * (i * xc)
    return a, u


def _lru_prompt_kernel(x_ref, gate_ref, cw_ref, cb_ref, wa_ref, ba_ref, wx_ref, bx_ref, sp_ref,
                       y_ref, hfin_ref, h_scr, tail_scr, a_scr, u_scr, hs_scr, *, T):
    c = pl.program_id(1)

    @pl.when(c == 0)
    def _():
        h_scr[...] = jnp.zeros_like(h_scr)
        tail_scr[...] = jnp.zeros_like(tail_scr)

    x = x_ref[...]
    xx = jnp.concatenate([tail_scr[...], x], axis=0)
    base = 8 - (CONV_W - 1)
    xc = cb_ref[...] + sum(xx[base + j:base + j + T] * cw_ref[j:j + 1, :] for j in range(CONV_W))
    tail_scr[...] = x[T - 8:T]
    a, u = _lru_gates(xc, wa_ref, ba_ref, wx_ref, bx_ref, sp_ref)
    a_scr[...] = a
    u_scr[...] = u

    def group(gi, h):
        off = pl.multiple_of(gi * 8, 8)
        ag = a_scr[pl.ds(off, 8), :]
        ug = u_scr[pl.ds(off, 8), :]
        outs = []
        for j in range(8):
            h = ag[j:j + 1] * h + ug[j:j + 1]
            outs.append(h)
        hs_scr[pl.ds(off, 8), :] = jnp.concatenate(outs, axis=0)
        return h

    h = lax.fori_loop(0, T // 8, group, h_scr[...])
    h_scr[...] = h
    hfin_ref[...] = h
    y_ref[...] = hs_scr[...] * jax.nn.gelu(gate_ref[...])


def _lru_weights(conv_w, conv_b, w_a, b_a, w_x, b_x, lam):
    def bd(w):
        eye = jnp.eye(LRU_HEADS, dtype=w.dtype)
        return (w[:, :, None, :] * eye[:, None, :, None]).reshape(LRU_WIDTH, LRU_WIDTH)
    sp = jax.nn.softplus(-lam.astype(F32)).reshape(1, LRU_WIDTH)
    r1 = lambda t: t.reshape(1, LRU_WIDTH)
    return conv_w, r1(conv_b), bd(w_a), r1(b_a), bd(w_x), r1(b_x), sp


def _lru_prompt_call(lru, weights, T):
    bsz, L, _ = lru.shape
    W = LRU_WIDTH
    const = lambda shape: pl.BlockSpec(shape, lambda b, c: (0,) * len(shape))
    kern = functools.partial(_lru_prompt_kernel, T=T)
    return pl.pallas_call(
        kern,
        grid=(bsz, L // T),
        in_specs=[pl.BlockSpec((None, T, W), lambda b, c: (b, c, 0)),
                  pl.BlockSpec((None, T, W), lambda b, c: (b, c, 1)),
                  const((CONV_W, W)), const((1, W)), const((W, W)), const((1, W)),
                  const((W, W)), const((1, W)), const((1, W))],
        out_specs=[pl.BlockSpec((None, T, W), lambda b, c: (b, c, 0)),
                   pl.BlockSpec((None, 1, W), lambda b, c: (b, 0, 0))],
        out_shape=[jax.ShapeDtypeStruct((bsz, L, W), F32), jax.ShapeDtypeStruct((bsz, 1, W), F32)],
        scratch_shapes=[pltpu.VMEM((1, W), F32), pltpu.VMEM((8, W), F32), pltpu.VMEM((T, W), F32),
                        pltpu.VMEM((T, W), F32), pltpu.VMEM((T, W), F32)],
        compiler_params=_cparams(("parallel", "arbitrary")),
        name="lru_prompt",
    )(lru, lru, *weights)


def _lru_sample_kernel(x_ref, gate_ref, buf_ref, h0_ref, cw_ref, cb_ref, wa_ref, ba_ref, wx_ref, bx_ref, sp_ref,
                       y_ref, hfin_ref, *, L):
    xx = [buf_ref[j] for j in range(CONV_W - 1)] + [x_ref[t] for t in range(L)]
    h = h0_ref[...]
    for t in range(L):
        xc = cb_ref[...] + sum(xx[t + j] * cw_ref[j:j + 1, :] for j in range(CONV_W))
        a, u = _lru_gates(xc, wa_ref, ba_ref, wx_ref, bx_ref, sp_ref)
        h = a * h + u
        y_ref[t] = h * jax.nn.gelu(gate_ref[t])
    hfin_ref[...] = h


def _lru_sample_call(x_t, gate_t, buf_t, h0, weights):
    L, bsz, W = x_t.shape
    kern = functools.partial(_lru_sample_kernel, L=L)
    return pl.pallas_call(
        kern,
        out_shape=[jax.ShapeDtypeStruct((L, bsz, W), F32), jax.ShapeDtypeStruct((bsz, W), F32)],
        compiler_params=pltpu.CompilerParams(vmem_limit_bytes=VMEM_LIMIT),
        name="lru_sample",
    )(x_t, gate_t, buf_t, h0, *weights)


def _later_and_total(tk=ATT_BLK):
    later = np.arange(tk)[:, None] > np.arange(tk)[None, :]
    return jnp.asarray(np.concatenate([later, np.ones((tk, LANES), bool)], axis=1), BF16)


def _sb_logs(z, uo, mask):
    tk = z.shape[1]
    sp = _softplus(z)
    ls = -sp if mask is None else jnp.where(mask, -sp, 0.0)
    hi, lo = _split2(ls)
    res = _dg(hi, uo) + _dg(lo, uo)
    return z - sp, res[:, :tk], res[:, tk:]


def _head_rows(q, n_rows):
    headmask = (_iota((SB_HEADS * n_rows, HW), 0) // n_rows) == (_iota((SB_HEADS * n_rows, HW), 1) // HEAD_DIM)
    qs = jnp.where(headmask, jnp.concatenate([q * (HEAD_DIM ** -0.5)] * SB_HEADS, axis=0), 0.0)
    return qs.astype(BF16), headmask


def _pick_heads(acc, headmask, n_rows):
    return sum(jnp.where(headmask[h * n_rows:(h + 1) * n_rows], acc[h * n_rows:(h + 1) * n_rows], 0.0)
               for h in range(SB_HEADS))


def _sb_prompt_kernel(q_ref, kv_ref, uo_ref, uo2_ref, o_ref, acc_scr, cs_scr, *, BB):
    i = pl.program_id(1)
    Q = ATT_BLK
    R = SB_HEADS * Q
    heads = [_head_rows(q_ref[bi], Q) for bi in range(BB)]
    headmask = heads[0][1]

    def tile(start, tk, uo_ref_, mask):
        uo = uo_ref_[...]
        for bi in range(BB):
            kv = kv_ref[bi, pl.ds(pl.multiple_of(start, Q), tk), :]
            z = _dg(heads[bi][0], kv[:, :HW], NT)
            zl, btw, tot = _sb_logs(z, uo, mask)
            cs = cs_scr[bi]
            e = jnp.exp(zl + (jnp.concatenate([cs] * (tk // LANES), axis=1) + btw))
            w = e if mask is None else jnp.where(mask, e, 0.0)
            cs_scr[bi] = cs + tot
            acc_scr[bi] += _dg(w.astype(BF16), kv[:, HW:])

    acc_scr[...] = jnp.zeros_like(acc_scr)
    cs_scr[...] = jnp.zeros_like(cs_scr)
    tile(i * Q, Q, uo_ref, _iota((R, Q), 1) < _iota((R, Q), 0) % Q)

    @pl.when(i % 2 == 1)
    def _():
        tile((i - 1) * Q, Q, uo_ref, None)

    n_pairs = i // 2

    def alive():
        cs = functools.reduce(jnp.maximum, [cs_scr[bi] for bi in range(BB)])
        return jnp.max(jnp.max(cs, axis=0, keepdims=True)) > -SB_DEAD

    def body(c):
        tile((n_pairs - 1 - c[0]) * 2 * Q, 2 * Q, uo2_ref, None)
        return c[0] + 1, alive()

    lax.while_loop(lambda c: jnp.logical_and(c[0] < n_pairs, c[1]), body, (0, alive()))
    for bi in range(BB):
        o_ref[bi] = _pick_heads(acc_scr[bi], headmask, Q)


def _sb_prompt_call(sbq, sbkv_b):
    bsz, L, _ = sbq.shape
    R = SB_HEADS * ATT_BLK
    BB = _row_tile(bsz, 2)
    return pl.pallas_call(
        functools.partial(_sb_prompt_kernel, BB=BB),
        grid=(bsz // BB, L // ATT_BLK),
        in_specs=[pl.BlockSpec((BB, ATT_BLK, HW), lambda b, i: (b, i, 0)),
                  pl.BlockSpec((BB, L, 2 * HW), lambda b, i: (b, 0, 0)),
                  pl.BlockSpec((ATT_BLK, ATT_BLK + LANES), lambda b, i: (0, 0)),
                  pl.BlockSpec((2 * ATT_BLK, 2 * ATT_BLK + LANES), lambda b, i: (0, 0))],
        out_specs=pl.BlockSpec((BB, ATT_BLK, HW), lambda b, i: (b, i, 0)),
        out_shape=jax.ShapeDtypeStruct((bsz, L, HW), F32),
        scratch_shapes=[pltpu.VMEM((BB, R, HW), F32), pltpu.VMEM((BB, R, ATT_BLK), F32)],
        compiler_params=_cparams(("parallel", "arbitrary")),
        name="sb_prompt",
    )(sbq, sbkv_b, _later_and_total(), _later_and_total(2 * ATT_BLK))


def _sb_sample_kernel(pt_ref, alive_ref, q_ref, init_a_ref, init_b_ref, *rest, L, pps, first):
    page_refs = rest[:pps]
    uo_ref, o_ref, acc_ref, cs_ref = rest[pps:]
    b = pl.program_id(0)
    s = pl.program_id(1)
    R = SB_HEADS * L
    uo = uo_ref[...]
    q16, headmask = _head_rows(q_ref[...], L)

    @pl.when(s == 0)
    def _():
        if first:
            new = init_a_ref[...]
            mask = _iota((R, ATT_BLK), 1) < (_iota((R, ATT_BLK), 0) % L)
            zl, btw, tot = _sb_logs(_dg(q16, new[:, :HW].astype(BF16), NT), uo, mask)
            w = jnp.where(mask, jnp.exp(zl + btw), 0.0)
            cs_ref[...] = tot
            acc_ref[...] = _dg(w.astype(BF16), new[:, HW:].astype(BF16))
        else:
            acc_ref[...] = init_a_ref[...]
            cs_ref[...] = init_b_ref[...]

    @pl.when(alive_ref[b] > 0)
    def _():
        z = jnp.concatenate([_dg(q16, page_refs[r][:HW, :].astype(BF16)) for r in range(pps)], axis=0)
        zl, btw, tot = _sb_logs(z, uo, None)
        cs = cs_ref[...]
        acc = acc_ref[...]
        for r in range(pps):
            sl = slice(r * R, (r + 1) * R)
            w = jnp.exp(zl[sl] + (cs + btw[sl]))
            acc = acc + _dg(w.astype(BF16), page_refs[r][HW:, :].astype(BF16), NT)
            cs = cs + tot[sl]
        acc_ref[...] = acc
        cs_ref[...] = cs

    @pl.when(s == pl.num_programs(1) - 1)
    def _():
        o_ref[...] = _pick_heads(acc_ref[...], headmask, L)


def _sb_sample_stage(sbq, init_a, init_b, alive, cache_t, page_table, layer, pps, first):
    bsz, L, _ = sbq.shape
    n_pages = page_table.shape[1]
    steps = 1 if first else n_pages // pps - 1
    skip = 0 if first else 1
    R = SB_HEADS * L
    kern = functools.partial(_sb_sample_kernel, L=L, pps=pps, first=first)

    def page_spec(r):
        return pl.BlockSpec((None, None, 2 * HW, ATT_BLK),
                            lambda b, s, pt, al: (layer, jnp.where(al[b] > 0,
                                                                   pt[b, n_pages - 1 - ((s + skip) * pps + r)], 0),
                                                  0, 0))

    per_b = lambda shape: pl.BlockSpec((None,) + shape, lambda b, s, pt, al: (b, 0, 0))
    grid_spec = pltpu.PrefetchScalarGridSpec(
        num_scalar_prefetch=2,
        grid=(bsz, steps),
        in_specs=[per_b((L, HW)), per_b(init_a.shape[1:]), per_b(init_b.shape[1:])]
                 + [page_spec(r) for r in range(pps)]
                 + [pl.BlockSpec((ATT_BLK, 2 * ATT_BLK), lambda b, s, pt, al: (0, 0))],
        out_specs=[per_b((L, HW)), per_b((R, HW)), per_b((R, ATT_BLK))],
    )
    return pl.pallas_call(
        kern,
        grid_spec=grid_spec,
        out_shape=[jax.ShapeDtypeStruct((bsz, L, HW), F32), jax.ShapeDtypeStruct((bsz, R, HW), F32),
                   jax.ShapeDtypeStruct((bsz, R, ATT_BLK), F32)],
        compiler_params=_cparams(("parallel", "arbitrary")),
        name="sb_sample",
    )(page_table, alive, sbq, init_a, init_b, *([cache_t] * pps), _later_and_total())


def _sb_sample_call(sbq, new_kv, cache_t, page_table, layer, pps):
    bsz, L, _ = sbq.shape
    R = SB_HEADS * L
    y, acc, cs = _sb_sample_stage(sbq, new_kv, jnp.zeros((bsz, R, ATT_BLK), F32), jnp.ones((bsz,), jnp.int32),
                                  cache_t, page_table, layer, pps, True)
    if page_table.shape[1] // pps > 1:
        alive = (jnp.max(cs, axis=(1, 2)) > -SB_DEAD).astype(jnp.int32)
        y, _, _ = _sb_sample_stage(sbq, acc, cs, alive, cache_t, page_table, layer, pps, False)
    return y


def _masked_softmax(s, valid):
    sm = jnp.where(valid, s, NEG)
    e = jnp.where(valid, jnp.exp(sm - jnp.max(sm, axis=1, keepdims=True)), 0.0)
    return e / jnp.maximum(jnp.sum(e, axis=1, keepdims=True), 1e-30)


def _top_select(score, n_sel):
    T, N = score.shape
    midx = _iota((N, N), 0)
    nidx = _iota((N, N), 1)
    ranks = []
    for t in range(T):
        row = score[t:t + 1, :]
        col = jnp.sum(jnp.where(midx == nidx, row, 0.0), axis=1, keepdims=True)
        beats = jnp.where(col > row, 1.0, jnp.where((col == row) & (midx < nidx), 1.0, 0.0))
        ranks.append(jnp.sum(beats, axis=0, keepdims=True))
    return jnp.where(jnp.concatenate(ranks, axis=0) < n_sel, 1.0, 0.0)


def _nsa_prompt_kernel(q_ref, rows_ref, win_ref, misc_ref, ex_ref, o_ref,
                       kcvc_scr, kse_scr, vse_scr, kwe_scr, vwe_scr, score_scr, pick_scr, s_scr, mrun_scr,
                       m_scr, acc_scr, *, L):
    i = pl.program_id(1)
    nb = L // NSA_BLOCK
    Q = ATT_BLK
    G = NSA_HEADS
    R = G * Q

    @pl.when(i == 0)
    def _():
        blk = rows_ref[:, :2 * HEAD_DIM].reshape(nb, NSA_BLOCK, 2 * HEAD_DIM)
        kcvc_scr[...] = jnp.sum(blk, axis=1) * (1.0 / NSA_BLOCK)
        ksvs = rows_ref[:, 2 * HEAD_DIM:]
        kse_scr[...] = _low_half(ksvs).astype(BF16)
        vse_scr[...] = _low_half(_swap_halves(ksvs), 1.0).astype(BF16)
        kwvw = win_ref[...]
        kwe_scr[...] = _low_half(kwvw).astype(BF16)
        vwe_scr[...] = _low_half(_swap_halves(kwvw), 1.0).astype(BF16)

    q4 = (_heads_to_rows(q_ref[...]) * (HEAD_DIM ** -0.5)).astype(BF16)
    qpos1 = i * Q + _iota((Q, 1), 0)
    qpos4 = jnp.concatenate([qpos1] * G, axis=0)

    kcvc = kcvc_scr[...]
    s_c = _dg(q4, _low_half(kcvc).astype(BF16), NT)
    valid_c = (_iota((R, nb), 1) + 1) * NSA_BLOCK - 1 <= qpos4
    p_c = _masked_softmax(s_c, valid_c)
    o_c = _dg(p_c.astype(BF16), _swap_halves(kcvc).astype(BF16))

    imp = sum(p_c[h * Q:(h + 1) * Q] for h in range(G))
    blk1 = _iota((Q, nb), 1)
    score = jnp.where(blk1 == qpos1 // NSA_BLOCK, NSA_FORCE, jnp.where(blk1 * NSA_BLOCK <= qpos1, imp, -1.0))
    score_t = score.T
    score_scr[...] = score_t
    nidx = _iota((nb, Q), 0)

    def rank_body(m, rank):
        row = score_scr[pl.ds(m, 1), :]
        earlier = jnp.where(nidx > m, 1.0, 0.0)
        return rank + jnp.where(row > score_t, 1.0, jnp.where(row == score_t, earlier, 0.0))

    rank = lax.fori_loop(0, nb, rank_body, jnp.zeros((nb, Q), F32), unroll=8)
    sel_t = jnp.where(rank < min(NSA_TOPK, nb), 1.0, 0.0).astype(BF16)
    picked = _dg(sel_t, ex_ref[...], TN)
    causal = _iota((Q, L), 1) <= qpos1
    pick_scr[...] = jnp.where(causal & (picked > 0.5), 0.0, NEG)

    n_tiles = (i * Q + Q + SEL_TILE - 1) // SEL_TILE
    n_chunks = SEL_TILE // LANES
    mrun_scr[...] = jnp.full(mrun_scr.shape, NEG, F32)

    def pass1(j, c):
        off = pl.multiple_of(j * SEL_TILE, SEL_TILE)
        bias = pick_scr[:, pl.ds(off, SEL_TILE)]
        kb = kse_scr[pl.ds(off, SEL_TILE), :]
        for h in range(G):
            sl = slice(h * Q, (h + 1) * Q)
            sm = _dg(q4[sl], kb, NT) + bias
            s_scr[sl, pl.ds(off, SEL_TILE)] = sm
            mh = functools.reduce(jnp.maximum, [sm[:, k * LANES:(k + 1) * LANES] for k in range(n_chunks)])
            mrun_scr[sl, :] = jnp.maximum(mrun_scr[sl, :], mh)
        return c

    lax.fori_loop(0, n_tiles, pass1, 0)
    m_scr[...] = jnp.broadcast_to(jnp.max(mrun_scr[...], axis=1, keepdims=True), (R, LANES))
    acc_scr[...] = jnp.zeros_like(acc_scr)

    def pass2(j, c):
        off = pl.multiple_of(j * SEL_TILE, SEL_TILE)
        vb = vse_scr[pl.ds(off, SEL_TILE), :]
        for h in range(G):
            sl = slice(h * Q, (h + 1) * Q)
            p = jnp.exp(s_scr[sl, pl.ds(off, SEL_TILE)] - jnp.concatenate([m_scr[sl, :]] * n_chunks, axis=1))
            acc_scr[sl, :] += _dg(p.astype(BF16), vb)
        return c

    lax.fori_loop(0, n_tiles, pass2, 0)
    acc = acc_scr[...]
    o_s = acc / jnp.maximum(_swap_halves(acc), 1e-30)

    col = _iota((Q, ATT_BLK), 1)
    n_win = NSA_WINDOW // ATT_BLK + 1
    offs, biases = [], []
    for t in range(n_win):
        j = i - t
        offs.append(pl.multiple_of(jnp.maximum(j, 0) * ATT_BLK, ATT_BLK))
        kpos = j * ATT_BLK + col
        dist = qpos1 - kpos
        biases.append(jnp.where((dist >= 0) & (dist <= NSA_WINDOW) & (kpos >= 0), 0.0, NEG))
    bias_w = jnp.concatenate(biases, axis=1)
    kw = jnp.concatenate([kwe_scr[pl.ds(o, ATT_BLK), :] for o in offs], axis=0)
    vw = jnp.concatenate([vwe_scr[pl.ds(o, ATT_BLK), :] for o in offs], axis=0)
    o_w_parts = []
    for h in range(G):
        sm = _dg(q4[h * Q:(h + 1) * Q], kw, NT) + bias_w
        p = jnp.exp(sm - jnp.max(sm, axis=1, keepdims=True))
        o_w_parts.append(_dg(p.astype(BF16), vw))
    acc = jnp.concatenate(o_w_parts, axis=0)
    o_w = acc / jnp.maximum(_swap_halves(acc), 1e-30)

    gates = jax.nn.sigmoid(misc_ref[:, GLA_RANK:GLA_RANK + 3 * G])
    parts = []
    for h in range(G):
        sl = slice(h * Q, (h + 1) * Q)
        parts.append(gates[:, 3 * h:3 * h + 1] * o_c[sl] + gates[:, 3 * h + 1:3 * h + 2] * o_s[sl]
                     + gates[:, 3 * h + 2:3 * h + 3] * o_w[sl])
    o_ref[...] = _rows_to_heads(parts)


def _nsa_prompt_call(nq, rows, win, misc):
    bsz, L, _ = nq.shape
    nb = L // NSA_BLOCK
    R = NSA_HEADS * ATT_BLK
    expand = jnp.asarray(np.arange(nb)[:, None] == np.arange(L)[None, :] // NSA_BLOCK, BF16)
    kern = functools.partial(_nsa_prompt_kernel, L=L)
    return pl.pallas_call(
        kern,
        grid=(bsz, L // ATT_BLK),
        in_specs=[pl.BlockSpec((None, ATT_BLK, HW), lambda b, i: (b, i, 0)),
                  pl.BlockSpec((None, L, HW), lambda b, i: (b, 0, 0)),
                  pl.BlockSpec((None, L, 2 * HEAD_DIM), lambda b, i: (b, 0, 0)),
                  pl.BlockSpec((None, ATT_BLK, LANES), lambda b, i: (b, i, 0)),
                  pl.BlockSpec((nb, L), lambda b, i: (0, 0))],
        out_specs=pl.BlockSpec((None, ATT_BLK, HW), lambda b, i: (b, i, 0)),
        out_shape=jax.ShapeDtypeStruct((bsz, L, HW), F32),
        scratch_shapes=[pltpu.VMEM((nb, LANES), F32),
                        pltpu.VMEM((L, LANES), BF16), pltpu.VMEM((L, LANES), BF16),
                        pltpu.VMEM((L, LANES), BF16), pltpu.VMEM((L, LANES), BF16),
                        pltpu.VMEM((nb, ATT_BLK), F32), pltpu.VMEM((ATT_BLK, L), F32),
                        pltpu.VMEM((R, L), F32), pltpu.VMEM((R, LANES), F32),
                        pltpu.VMEM((R, LANES), F32), pltpu.VMEM((R, LANES), F32)],
        compiler_params=_cparams(("parallel", "arbitrary")),
        name="nsa_prompt",
    )(nq, rows, win, misc, expand)


def _nsa_sample_kernel(pt_ref, q_ref, new_ref, winp_ref, winn_ref, misc_ref, ea_ref, eb_ref, *rest,
                       L, pps, n_pages):
    page_refs = rest[:pps]
    o_ref, kcvc_scr, ksvs_scr = rest[pps:]
    s = pl.program_id(1)
    steps = n_pages // pps
    G = NSA_HEADS
    R = G * L
    per_page = ATT_BLK // NSA_BLOCK
    bps = pps * per_page
    gk = pps * ATT_BLK
    nbp = n_pages * per_page
    nbp_pad = kcvc_scr.shape[1]
    q4 = (_heads_to_rows(q_ref[...]) * (HEAD_DIM ** -0.5)).astype(BF16)
    t_row = _iota((R, 1), 0) % L

    @pl.when(s == 0)
    def _():
        kcvc_scr[...] = jnp.zeros_like(kcvc_scr)

    x = jnp.concatenate([page_refs[r][:2 * HEAD_DIM, :] for r in range(pps)], axis=1)
    means = _dot_xl(x, ea_ref[...])
    base = (s * bps) % LANES
    off = pl.multiple_of(((s * bps) // LANES) * LANES, LANES)
    kcvc_scr[:, pl.ds(off, LANES)] += pltpu.roll(means, base, 1)
    for r in range(pps):
        koff = pl.multiple_of((s * pps + r) * ATT_BLK, ATT_BLK)
        ksvs_scr[:, pl.ds(koff, ATT_BLK)] = page_refs[r][2 * HEAD_DIM:, :].astype(BF16)

    @pl.when(s == steps - 1)
    def _():
        kcvc = kcvc_scr[...].astype(BF16)
        live = _iota((R, nbp_pad), 1) < nbp
        s_c = _dg(q4, kcvc)
        p_c = _masked_softmax(s_c, live)
        o_c = _swap_halves(_dg(p_c.astype(BF16), kcvc, NT))
        imp = sum(p_c[h * L:(h + 1) * L] for h in range(G))
        imp = jnp.where(live[:L], imp, -3.0)
        sel = _top_select(imp, min(NSA_TOPK, nbp + 1) - 1)
        sel = jnp.concatenate([sel] * G, axis=0).astype(BF16)
        n_buf = winp_ref.shape[1]
        winp = winp_ref[...].astype(BF16)
        winn = winn_ref[...]
        dist_p = t_row + n_buf - _iota((R, n_buf), 1)
        dist_n = t_row - _iota((R, ATT_BLK), 1)
        s_w = jnp.concatenate([_dg(q4, winp), _dg(q4, _low_half(winn).astype(BF16), NT)], axis=1)
        valid_w = jnp.concatenate([(dist_p >= 0) & (dist_p <= NSA_WINDOW),
                                   (dist_n >= 0) & (dist_n <= NSA_WINDOW)], axis=1)
        p_w = _masked_softmax(s_w, valid_w).astype(BF16)
        o_w = _swap_halves(_dg(p_w[:, :n_buf], winp, NT) + _dg(p_w[:, n_buf:], winn.astype(BF16)))
        ksvs_new = new_ref[:, 2 * HEAD_DIM:]
        sms = [jnp.where(dist_n >= 0, _dg(q4, _low_half(ksvs_new).astype(BF16), NT), NEG)]
        for g in range(steps):
            picked = _dg(sel[:, g * bps:(g + 1) * bps], eb_ref[...]) > 0.5
            sms.append(jnp.where(picked, _dg(q4, ksvs_scr[:, g * gk:(g + 1) * gk]), NEG))
        m = functools.reduce(jnp.maximum, [jnp.max(t, axis=1, keepdims=True) for t in sms])
        p0 = jnp.exp(sms[0] - m)
        l = jnp.sum(p0, axis=1, keepdims=True)
        acc = _dg(p0.astype(BF16), ksvs_new.astype(BF16))
        for g in range(steps):
            p = jnp.exp(sms[g + 1] - m)
            l = l + jnp.sum(p, axis=1, keepdims=True)
            acc = acc + _dg(p.astype(BF16), ksvs_scr[:, g * gk:(g + 1) * gk], NT)
        o_s = _swap_halves(acc) / jnp.maximum(l, 1e-30)
        gates = jax.nn.sigmoid(misc_ref[:, GLA_RANK:GLA_RANK + 3 * G])
        parts = []
        for h in range(G):
            sl = slice(h * L, (h + 1) * L)
            parts.append(gates[:, 3 * h:3 * h + 1] * o_c[sl] + gates[:, 3 * h + 1:3 * h + 2] * o_s[sl]
                         + gates[:, 3 * h + 2:3 * h + 3] * o_w[sl])
        o_ref[...] = _rows_to_heads(parts)


def _nsa_sample_call(nq, new_rows, win_past_t, win_new, misc, cache_t, page_table, layer, pps):
    bsz, L, _ = nq.shape
    n_pages = page_table.shape[1]
    steps = n_pages // pps
    n_buf = win_past_t.shape[3]
    per_page = ATT_BLK // NSA_BLOCK
    bps = pps * per_page
    nbp = n_pages * per_page
    nbp_pad = -(-nbp // LANES) * LANES
    assert LANES % bps == 0
    R = NSA_HEADS * L
    keys = np.arange(pps * ATT_BLK)
    ea = jnp.asarray((keys[:, None] // NSA_BLOCK == np.arange(LANES)[None, :]) / float(NSA_BLOCK), BF16)
    eb = jnp.asarray(np.arange(bps)[:, None] == keys[None, :] // NSA_BLOCK, BF16)
    kern = functools.partial(_nsa_sample_kernel, L=L, pps=pps, n_pages=n_pages)

    def page_spec(r):
        return pl.BlockSpec((None, None, HW, ATT_BLK), lambda b, s, pt: (layer, pt[b, s * pps + r], 0, 0))

    cst = lambda shape: pl.BlockSpec(shape, lambda b, s, pt: (0,) * len(shape))
    grid_spec = pltpu.PrefetchScalarGridSpec(
        num_scalar_prefetch=1,
        grid=(bsz, steps),
        in_specs=[pl.BlockSpec((None, L, HW), lambda b, s, pt: (b, 0, 0)),
                  pl.BlockSpec((None, ATT_BLK, HW), lambda b, s, pt: (b, 0, 0)),
                  pl.BlockSpec((None, None, 2 * HEAD_DIM, n_buf), lambda b, s, pt: (layer, b, 0, 0)),
                  pl.BlockSpec((None, ATT_BLK, 2 * HEAD_DIM), lambda b, s, pt: (b, 0, 0)),
                  pl.BlockSpec((None, L, LANES), lambda b, s, pt: (b, 0, 0)),
                  cst((pps * ATT_BLK, LANES)), cst((bps, pps * ATT_BLK))]
                 + [page_spec(r) for r in range(pps)],
        out_specs=pl.BlockSpec((None, L, HW), lambda b, s, pt: (b, 0, 0)),
        scratch_shapes=[pltpu.VMEM((2 * HEAD_DIM, nbp_pad), F32),
                        pltpu.VMEM((2 * HEAD_DIM, n_pages * ATT_BLK), BF16)],
    )
    return pl.pallas_call(
        kern,
        grid_spec=grid_spec,
        out_shape=jax.ShapeDtypeStruct((bsz, L, HW), F32),
        compiler_params=_cparams(("parallel", "arbitrary")),
        name="nsa_sample",
    )(page_table, nq, new_rows, win_past_t, win_new, misc, ea, eb, *([cache_t] * pps))


def _outproj_kernel(ya_ref, yb_ref, yc_ref, yd_ref, x_ref, w_ref, g_ref, b_ref, o_ref):
    y = jnp.concatenate([ya_ref[...], yb_ref[...], yc_ref[...], yd_ref[...]], axis=1).astype(BF16)
    h = jnp.dot(y, w_ref[...], preferred_element_type=F32)
    o_ref[...] = _layer_norm(DN_ALPHA * x_ref[...] + h, g_ref[...], b_ref[...])


def _outproj_call(ya, yb, yc, yd, x, w, g, b, tm):
    n = x.shape[0]
    part = lambda: pl.BlockSpec((tm, HW), lambda i: (i, 0))
    return pl.pallas_call(
        _outproj_kernel,
        grid=(n // tm,),
        in_specs=[part(), part(), part(), part(),
                  pl.BlockSpec((tm, D_MODEL), lambda i: (i, 0)),
                  pl.BlockSpec((4 * HW, D_MODEL), lambda i: (0, 0)),
                  pl.BlockSpec((1, D_MODEL), lambda i: (0, 0)),
                  pl.BlockSpec((1, D_MODEL), lambda i: (0, 0))],
        out_specs=pl.BlockSpec((tm, D_MODEL), lambda i: (i, 0)),
        out_shape=jax.ShapeDtypeStruct((n, D_MODEL), F32),
        compiler_params=_cparams(("parallel",)),
        name="out_proj_ln",
    )(ya, yb, yc, yd, x, w, g.reshape(1, D_MODEL), b.reshape(1, D_MODEL))


def _route(logits):
    lane = _iota(logits.shape, 1)
    live = lane < N_EXPERTS
    lg = jnp.where(live, logits, NEG)
    e = jnp.where(live, jnp.exp(lg - jnp.max(lg, axis=1, keepdims=True)), 0.0)
    probs = e / jnp.sum(e, axis=1, keepdims=True)
    big = logits.shape[1]

    def top2(vals):
        w1 = jnp.max(vals, axis=1, keepdims=True)
        i1 = jnp.min(jnp.where(vals == w1, lane, big), axis=1, keepdims=True)
        rest = jnp.where(lane == i1, -2.0, vals)
        w2 = jnp.max(rest, axis=1, keepdims=True)
        i2 = jnp.min(jnp.where(rest == w2, lane, big), axis=1, keepdims=True)
        return w1, i1, w2, i2

    best = None
    g_sel = None
    for g in range(N_GROUPS):
        in_g = (lane // EXPERTS_PER_GROUP) == g
        w1, _, w2, _ = top2(jnp.where(in_g, probs, -1.0))
        tot = w1 + w2
        if best is None:
            best, g_sel = tot, jnp.zeros_like(tot, dtype=jnp.int32)
        else:
            upd = tot > best
            g_sel = jnp.where(upd, g, g_sel)
            best = jnp.where(upd, tot, best)
    in_grp = live & ((lane // EXPERTS_PER_GROUP) == g_sel)
    w1, i1, w2, i2 = top2(jnp.where(in_grp, probs, -1.0))
    den = w1 + w2
    return jnp.where(lane == i1, w1 / den, 0.0) + jnp.where(lane == i2, w2 / den, 0.0)


def _moe_kernel(x_ref, rw_ref, wg_ref, wu_ref, wd_ref, g_ref, b_ref, o_ref, acc_scr, comb_scr, xb_scr):
    e = pl.program_id(1)

    @pl.when(e == 0)
    def _():
        x = x_ref[...]
        xb_scr[...] = x.astype(BF16)
        comb_scr[...] = _route(_dot3(x, rw_ref[...]))
        acc_scr[...] = jnp.zeros_like(acc_scr)

    xb = xb_scr[...]
    hg = jnp.dot(xb, wg_ref[...], preferred_element_type=F32)
    hu = jnp.dot(xb, wu_ref[...], preferred_element_type=F32)
    comb = comb_scr[...]
    c = jnp.sum(jnp.where(_iota(comb.shape, 1) == e, comb, 0.0), axis=1, keepdims=True)
    h = (hg * jax.nn.sigmoid(hg)) * hu * c
    acc_scr[...] += jnp.dot(h.astype(BF16), wd_ref[...], preferred_element_type=F32)

    @pl.when(e == pl.num_programs(1) - 1)
    def _():
        o_ref[...] = _layer_norm(DN_ALPHA * x_ref[...] + acc_scr[...], g_ref[...], b_ref[...])


def _moe_call(x, rw, wg, wu, wd, g, b, tm):
    n = x.shape[0]
    return pl.pallas_call(
        _moe_kernel,
        grid=(n // tm, N_EXPERTS),
        in_specs=[pl.BlockSpec((tm, D_MODEL), lambda i, e: (i, 0)),
                  pl.BlockSpec((D_MODEL, LANES), lambda i, e: (0, 0)),
                  pl.BlockSpec((None, D_MODEL, D_EXPERT), lambda i, e: (e, 0, 0)),
                  pl.BlockSpec((None, D_MODEL, D_EXPERT), lambda i, e: (e, 0, 0)),
                  pl.BlockSpec((None, D_EXPERT, D_MODEL), lambda i, e: (e, 0, 0)),
                  pl.BlockSpec((1, D_MODEL), lambda i, e: (0, 0)),
                  pl.BlockSpec((1, D_MODEL), lambda i, e: (0, 0))],
        out_specs=pl.BlockSpec((tm, D_MODEL), lambda i, e: (i, 0)),
        out_shape=jax.ShapeDtypeStruct((n, D_MODEL), F32),
        scratch_shapes=[pltpu.VMEM((tm, D_MODEL), F32), pltpu.VMEM((tm, LANES), F32),
                        pltpu.VMEM((tm, D_MODEL), BF16)],
        compiler_params=_cparams(("parallel", "arbitrary")),
        name="moe_ln",
    )(x, rw, wg, wu, wd, g.reshape(1, D_MODEL), b.reshape(1, D_MODEL))


def _pad_rows(t, n):
    return jnp.pad(t, ((0, 0), (0, n - t.shape[1]), (0, 0)))


def _row_tile(n, pref):
    t = min(pref, n)
    while n % t:
        t //= 2
    return t


def _keys_minor(cache):
    d, p, rows = cache.shape[:3]
    return jnp.transpose(cache, (0, 1, 3, 4, 5, 2)).reshape(d, p, -1, rows)


def kernel(x_prompt, x_sample, cache_sb_kv, cache_nsa_kv, cache_nsa_win, state_gla, state_lru_h, state_lru_conv, page_table, ln_in_g, ln_in_b, w_in, gla_w_a2, gla_b_a, gla_norm_g, lru_conv_w, lru_conv_b, lru_w_a, lru_b_a, lru_w_x, lru_b_x, lru_lambda, w_out, ln1_g, ln1_b, router_w, moe_w_gate, moe_w_up, moe_w_down, ln2_g, ln2_b):
    bp, lp, d = x_prompt.shape
    bs, ls, _ = x_sample.shape
    depth = w_in.shape[0]
    page = cache_sb_kv.shape[2]
    n_pages = page_table.shape[1]
    past_len = n_pages * page
    n_buf = cache_nsa_win.shape[2]
    n_p, n_s = bp * lp, bs * ls
    assert page == ATT_BLK and lp % SEL_TILE == 0 and ls <= 8 and d == D_MODEL
    assert n_pages * (ATT_BLK // NSA_BLOCK) >= NSA_TOPK - 1 and n_buf <= past_len

    tm_p = _row_tile(n_p, 512)
    tm_s = _row_tile(n_s, 512)
    tab_p = _rope_table(jnp.arange(lp))
    tab_s = jnp.tile(_rope_table(past_len + jnp.arange(ls)), (bs, 1))
    assert lp % tm_p == 0 and tm_s == n_s

    cache_sb_t = _keys_minor(cache_sb_kv)
    cache_nsa_t = _keys_minor(cache_nsa_kv)
    win_past_t = _keys_minor(cache_nsa_win)
    win_past = cache_nsa_win.reshape(depth, bs, n_buf, 2 * HEAD_DIM)
    rw = jnp.pad(router_w, ((0, 0), (0, LANES - N_EXPERTS)))
    pps = 16
    while n_pages % pps:
        pps //= 2
    gla_c = 32
    lru_t = _row_tile(lp, 512)

    xp = _ln_call(x_prompt.reshape(n_p, d), ln_in_g, ln_in_b, tm_p)
    xs = _ln_call(x_sample.reshape(n_s, d), ln_in_g, ln_in_b, tm_s)

    st_p, st_s = [], []
    for l in range(depth):
        w_l = _prep_w_in(w_in[l])
        lru_w = _lru_weights(lru_conv_w[l], lru_conv_b[l], lru_w_a[l], lru_b_a[l], lru_w_x[l], lru_b_x[l],
                             lru_lambda[l])
        w_out_l = w_out[l].astype(BF16)
        wg, wu, wd = moe_w_gate[l].astype(BF16), moe_w_up[l].astype(BF16), moe_w_down[l].astype(BF16)

        gla, lru, sbq, sbkv, nq, nsa, win, misc, sbkv_b = _proj_call(xp, w_l, tab_p, tm_p)
        r3 = lambda t, b_, l_: t.reshape(b_, l_, t.shape[-1])
        gla3, lru3, sbq3, sbkv3 = r3(gla, bp, lp), r3(lru, bp, lp), r3(sbq, bp, lp), r3(sbkv, bp, lp)
        nq3, nsa3, win3, misc3 = r3(nq, bp, lp), r3(nsa, bp, lp), r3(win, bp, lp), r3(misc, bp, lp)
        y_a, gla_t = _gla_call(gla3, misc3, gla_w_a2[l], gla_b_a[l], gla_norm_g[l],
                               jnp.zeros((bp, GV_W, QK_W), F32), gla_c, gla_c)
        y_b, lru_h = _lru_prompt_call(lru3, lru_w, lru_t)
        y_c = _sb_prompt_call(sbq3, sbkv_b.reshape(bp, lp, 2 * HW))
        y_d = _nsa_prompt_call(nq3, nsa3, win3, misc3)
        x1 = _outproj_call(y_a.reshape(n_p, HW), y_b.reshape(n_p, HW), y_c.reshape(n_p, HW), y_d.reshape(n_p, HW),
                           xp, w_out_l, ln1_g[l], ln1_b[l], tm_p)
        xp = _moe_call(x1, rw, wg, wu, wd, ln2_g[l], ln2_b[l], _row_tile(n_p, 1024))
        wn = min(NSA_WINDOW, lp)
        st_p.append((sbkv3.reshape(bp, lp, 2, SB_HEADS, HEAD_DIM), nsa3.reshape(bp, lp, 4, 1, HEAD_DIM),
                     win3[:, lp - wn:].reshape(bp, wn, 2, 1, HEAD_DIM), _gla_state_from_t(gla_t),
                     lru_h.reshape(bp, LRU_WIDTH), lru3[:, lp - (CONV_W - 1):, :LRU_WIDTH]))

        gla, lru, sbq, sbkv, nq, nsa, win, misc, _ = _proj_call(xs, w_l, tab_s, tm_s)
        gla3, lru3, sbq3, sbkv3 = r3(gla, bs, ls), r3(lru, bs, ls), r3(sbq, bs, ls), r3(sbkv, bs, ls)
        nq3, nsa3, win3, misc3 = r3(nq, bs, ls), r3(nsa, bs, ls), r3(win, bs, ls), r3(misc, bs, ls)
        gc = 16
        y_a, gla_t = _gla_call(_pad_rows(gla3, gc), _pad_rows(misc3, gc), gla_w_a2[l], gla_b_a[l], gla_norm_g[l],
                               _gla_state_to_t(state_gla[l].astype(F32)), gc, ls)
        y_a = y_a[:, :ls]
        tmaj = lambda t: jnp.swapaxes(t, 0, 1)
        y_b, lru_h = _lru_sample_call(tmaj(lru3[:, :, :LRU_WIDTH]), tmaj(lru3[:, :, LRU_WIDTH:]),
                                      tmaj(state_lru_conv[l]), state_lru_h[l].astype(F32), lru_w)
        y_b = tmaj(y_b)
        y_c = _sb_sample_call(sbq3, _pad_rows(sbkv3, ATT_BLK), cache_sb_t, page_table, l, pps)
        y_d = _nsa_sample_call(nq3, _pad_rows(nsa3, ATT_BLK), win_past_t, _pad_rows(win3, ATT_BLK), misc3,
                               cache_nsa_t, page_table, l, pps)
        x1 = _outproj_call(y_a.reshape(n_s, HW), y_b.reshape(n_s, HW), y_c.reshape(n_s, HW), y_d.reshape(n_s, HW),
                           xs, w_out_l, ln1_g[l], ln1_b[l], tm_s)
        xs = _moe_call(x1, rw, wg, wu, wd, ln2_g[l], ln2_b[l], tm_s)
        win_all = jnp.concatenate([win_past[l], win3], axis=1)
        lru_buf = jnp.concatenate([state_lru_conv[l], lru3[:, :, :LRU_WIDTH]], axis=1)[:, ls:]
        st_s.append((sbkv3.reshape(bs, ls, 2, SB_HEADS, HEAD_DIM), nsa3.reshape(bs, ls, 4, 1, HEAD_DIM),
                     win_all[:, win_all.shape[1] - n_buf:].reshape(bs, n_buf, 2, 1, HEAD_DIM),
                     _gla_state_from_t(gla_t), lru_h, lru_buf))

    outs_p = [jnp.stack([st[i] for st in st_p]) for i in range(6)]
    outs_s = [jnp.stack([st[i] for st in st_s]) for i in range(6)]
    res = [xp.reshape(bp, lp, d), xs.reshape(bs, ls, d)]
    for a, b in zip(outs_p, outs_s):
        res += [a, b]
    return tuple(res)
```

```python
import functools

import jax
import jax.numpy as jnp
import numpy as np
from jax import lax
from jax.experimental import pallas as pl
from jax.experimental.pallas import tpu as pltpu

F32 = jnp.float32
BF16 = jnp.bfloat16

D_MODEL = 1024
HEAD_DIM = 64
GLA_HEADS, GLA_DK, GLA_DV, GLA_RANK, GLA_TAU = 4, 32, 64, 16, 16.0
LRU_WIDTH, LRU_HEADS, CONV_W, LRU_C = 256, 4, 4, 8.0
LRU_BLOCK = LRU_WIDTH // LRU_HEADS
SB_HEADS = 4
NSA_HEADS, NSA_BLOCK, NSA_TOPK, NSA_WINDOW, NSA_FORCE = 4, 64, 16, 512, 1.0e4
ROPE_THETA, ROPE_DIM = 500000.0, HEAD_DIM // 4
N_EXPERTS, N_GROUPS, TOP_K, D_EXPERT = 16, 4, 2, 512
EXPERTS_PER_GROUP = N_EXPERTS // N_GROUPS
DEPTH = 2
DN_ALPHA = (2.0 * DEPTH) ** 0.25
EPS = 1e-5
NEG = -1e30

QK_W = GLA_HEADS * GLA_DK
GV_W = GLA_HEADS * GLA_DV
HW = 4 * HEAD_DIM
IN_SIZES = (QK_W, QK_W, GV_W, GV_W, GLA_RANK, LRU_WIDTH, LRU_WIDTH, HW, HW, HW, HW) + (HEAD_DIM,) * 6 + (12,)
IN_OFFSETS = tuple(int(v) for v in np.cumsum(IN_SIZES)[:-1])
P_GLA, P_LRU, P_SBQ, P_SBKV, P_NQ, P_NSA, P_WIN, P_MISC = 0, 768, 1280, 1536, 2048, 2304, 2560, 2688
P_TOTAL = 2816
LANES = 128
ATT_BLK = 128
SEL_TILE = 4 * ATT_BLK
NSA_Q = 2 * ATT_BLK
VMEM_LIMIT = 56 * 1024 * 1024
SB_DEAD = 105.0

NN = (((1,), (0,)), ((), ()))
NT = (((1,), (1,)), ((), ()))
TN = (((0,), (0,)), ((), ()))


def _cparams(sem):
    return pltpu.CompilerParams(dimension_semantics=sem, vmem_limit_bytes=VMEM_LIMIT)


def _split2(a):
    hi = a.astype(BF16)
    lo = (a - hi.astype(F32)).astype(BF16)
    return hi, lo


def _split3(a):
    hi = a.astype(BF16)
    r = a - hi.astype(F32)
    mid = r.astype(BF16)
    lo = (r - mid.astype(F32)).astype(BF16)
    return hi, mid, lo


def _dg(a, b, dims=NN):
    return lax.dot_general(a, b, dims, preferred_element_type=F32)


def _dot1(a, b, dims=NN):
    return _dg(a.astype(BF16), b.astype(BF16), dims)


def _dot3(a, b, dims=NN):
    ah, al = _split2(a)
    bh, bl = _split2(b)
    return _dg(ah, bh, dims) + (_dg(ah, bl, dims) + _dg(al, bh, dims))


def _dot_xl(a, b_exact, dims=NN):
    h, m, l = _split3(a)
    return _dg(h, b_exact, dims) + (_dg(m, b_exact, dims) + _dg(l, b_exact, dims))


def _dot_lx(a_exact, b, dims=NN):
    h, m, l = _split3(b)
    return _dg(a_exact, h, dims) + (_dg(a_exact, m, dims) + _dg(a_exact, l, dims))


def _softplus(x):
    return jnp.maximum(x, 0.0) + jnp.log(1.0 + jnp.exp(-jnp.abs(x)))


def _log_sigmoid(x):
    return -_softplus(-x)


def _layer_norm(x, g, b):
    mu = jnp.mean(x, axis=-1, keepdims=True)
    xc = x - mu
    var = jnp.mean(xc * xc, axis=-1, keepdims=True)
    return xc * lax.rsqrt(var + EPS) * g + b


def _iota(shape, dim):
    return lax.broadcasted_iota(jnp.int32, shape, dim)


def _low_half(x, fill=0.0):
    return jnp.where(_iota(x.shape, 1) < HEAD_DIM, x, fill)


def _swap_halves(x):
    return pltpu.roll(x, HEAD_DIM, 1)


def _heads_to_rows(q):
    pieces = []
    for h in range(HW // HEAD_DIM):
        pair = q[:, (h // 2) * LANES:(h // 2 + 1) * LANES]
        pieces.append(_low_half(pair if h % 2 == 0 else _swap_halves(pair)))
    return jnp.concatenate(pieces, axis=0)


def _rows_to_heads(parts):
    pairs = [jnp.where(_iota(parts[0].shape, 1) < HEAD_DIM, parts[2 * p], _swap_halves(parts[2 * p + 1]))
             for p in range(len(parts) // 2)]
    return jnp.concatenate(pairs, axis=1)


def _ln_kernel(x_ref, g_ref, b_ref, o_ref):
    o_ref[...] = _layer_norm(x_ref[...], g_ref[...], b_ref[...])


def _ln_call(x, g, b, tm):
    n, d = x.shape
    return pl.pallas_call(
        _ln_kernel,
        grid=(n // tm,),
        in_specs=[pl.BlockSpec((tm, d), lambda i: (i, 0)),
                  pl.BlockSpec((1, d), lambda i: (0, 0)),
                  pl.BlockSpec((1, d), lambda i: (0, 0))],
        out_specs=pl.BlockSpec((tm, d), lambda i: (i, 0)),
        out_shape=jax.ShapeDtypeStruct((n, d), F32),
        compiler_params=_cparams(("parallel",)),
        name="ln_in",
    )(x, g.reshape(1, d), b.reshape(1, d))


def _rope(v, c, s, flags):
    ones = jnp.ones_like(c)
    zeros = jnp.zeros_like(s)
    cc = jnp.concatenate([c if f else ones for f in flags], axis=-1)
    ss = jnp.concatenate([s if f else zeros for f in flags], axis=-1)
    n = v.shape[-1]
    half = ROPE_DIM // 2
    lane = _iota(v.shape, 1) % HEAD_DIM
    sw = jnp.where(lane < half, pltpu.roll(v, n - half, 1), pltpu.roll(v, half, 1))
    return v * cc + sw * ss


def _proj_kernel(x_ref, w_ref, tab_ref, gla_ref, lru_ref, sbq_ref, sbkv_ref, nq_ref, nsa_ref, win_ref, misc_ref,
                 sbkvb_ref):
    x = x_ref[...].astype(BF16)

    def mm(off, width):
        return jnp.dot(x, w_ref[:, off:off + width], preferred_element_type=F32)

    gla_ref[...] = mm(P_GLA, 768)
    lru_ref[...] = mm(P_LRU, 512)
    sbq_ref[...] = mm(P_SBQ, 256)
    sbkv = mm(P_SBKV, 512)
    sbkv_ref[...] = sbkv
    sbkvb_ref[...] = sbkv.astype(BF16)
    tab = tab_ref[...]
    c = tab[:, :HEAD_DIM]
    s = tab[:, HEAD_DIM:]
    nq_ref[...] = _rope(mm(P_NQ, 256), c, s, (1, 1, 1, 1))
    nsa_ref[...] = _rope(mm(P_NSA, 256), c, s, (1, 0, 1, 0))
    win_ref[...] = _rope(mm(P_WIN, 128), c, s, (1, 0))
    misc_ref[...] = mm(P_MISC, 128)


def _proj_call(x, w, tab, tm):
    n = x.shape[0]
    tab_blocks = tab.shape[0] // tm
    widths = (768, 512, 256, 512, 256, 256, 128, 128)
    return pl.pallas_call(
        _proj_kernel,
        grid=(n // tm,),
        in_specs=[pl.BlockSpec((tm, D_MODEL), lambda i: (i, 0)),
                  pl.BlockSpec((D_MODEL, P_TOTAL), lambda i: (0, 0)),
                  pl.BlockSpec((tm, LANES), lambda i: (i % tab_blocks, 0))],
        out_specs=[pl.BlockSpec((tm, wd), lambda i: (i, 0)) for wd in widths + (512,)],
        out_shape=[jax.ShapeDtypeStruct((n, wd), F32) for wd in widths]
                  + [jax.ShapeDtypeStruct((n, 512), BF16)],
        compiler_params=_cparams(("parallel",)),
        name="in_proj",
    )(x, w, tab)


def _rope_table(pos):
    inv = ROPE_THETA ** (-jnp.arange(0, ROPE_DIM, 2, dtype=F32) / ROPE_DIM)
    ang = pos.astype(F32)[:, None] * inv[None, :]
    cos, sin = jnp.cos(ang), jnp.sin(ang)
    n = pos.shape[0]
    c = jnp.concatenate([cos, cos, jnp.ones((n, HEAD_DIM - ROPE_DIM), F32)], axis=-1)
    s = jnp.concatenate([-sin, sin, jnp.zeros((n, HEAD_DIM - ROPE_DIM), F32)], axis=-1)
    return jnp.concatenate([c, s], axis=-1)


def _prep_w_in(w):
    parts = jnp.split(w, IN_OFFSETS, axis=-1)
    (gq, gk, gv, gg, ga, lx, lg, sq, sk, sv, nq, nkc, nvc, nks, nvs, nkw, nvw, ngate) = parts
    pad = jnp.zeros((w.shape[0], LANES - GLA_RANK - 12), w.dtype)
    return jnp.concatenate([gq, gk, gv, gg, lx, lg, sq, sk, sv, nq, nkc, nvc, nks, nvs, nkw, nvw,
                            ga, ngate, pad], axis=-1).astype(BF16)


def _gla_kernel(q_ref, k_ref, v_ref, g_ref, misc_ref, wa2_ref, ba_ref, ng_ref, indv_ref, bd_ref, ones_ref,
                s0_ref, y_ref, st_ref, st_scr, b_scr, phi_scr, plo_scr, att_scr, *, C, Cv, BB):
    c = pl.program_id(1)

    @pl.when(c == 0)
    def _():
        st_scr[...] = s0_ref[...]

    rows = _iota((C, QK_W), 0)
    tril = (_iota((C, C), 0) >= _iota((C, C), 1)).astype(BF16)
    qs, bs = [], []
    for bi in range(BB):
        la = _log_sigmoid(_dot3(misc_ref[bi, :, :GLA_RANK], wa2_ref[...]) + ba_ref[...]) * (1.0 / GLA_TAU)
        if Cv < C:
            la = jnp.where(rows < Cv, la, 0.0)
        b = _dot_lx(tril, la)
        b_scr[bi] = b
        bs.append(b)
        qs.append(q_ref[bi] * (GLA_DK ** -0.5))

    def fill(s, carry):
        off = pl.multiple_of(s * C, C)
        for bi in range(BB):
            b_s = b_scr[bi, pl.ds(s, 1), :]
            k_s = k_ref[bi, pl.ds(s, 1), :]
            d = jnp.where(rows >= s, bs[bi] - b_s, -jnp.inf)
            hi, lo = _split2(qs[bi] * k_s * jnp.exp(d))
            phi_scr[bi, pl.ds(off, C), :] = hi
            plo_scr[bi, pl.ds(off, C), :] = lo
        return carry

    lax.fori_loop(0, Cv, fill, 0, unroll=2)
    indv = indv_ref[...]
    for bi in range(BB):
        att_scr[bi, :Cv * C, :] = _dg(phi_scr[bi, :Cv * C, :], indv) + _dg(plo_scr[bi, :Cv * C, :], indv)

    def gather(s, os):
        off = pl.multiple_of(s * C, C)
        return tuple(os[bi] + att_scr[bi, pl.ds(off, C), :] * v_ref[bi, pl.ds(s, 1), :] for bi in range(BB))

    os = lax.fori_loop(0, Cv, gather, tuple(jnp.zeros((C, GV_W), F32) for _ in range(BB)), unroll=2)
    for bi in range(BB):
        b = bs[bi]
        st = st_scr[bi]
        o = os[bi] + _dot3(qs[bi] * jnp.exp(b), st, NT)
        bl = b[C - 1:C, :]
        kd = k_ref[bi] * jnp.exp(bl - b)
        upd = _dot3(v_ref[bi], kd, TN) * bd_ref[...]
        st_new = st * jnp.exp(bl) + upd
        st_scr[bi] = st_new
        st_ref[bi] = st_new
        ms = _dot_xl(o * o, ones_ref[...])
        on = o * lax.rsqrt(ms + EPS) * ng_ref[...]
        g = g_ref[bi]
        y_ref[bi] = on * (g * jax.nn.sigmoid(g))


def _gla_call(gla, misc, wa2, ba, ng, s0_t, C, Cv):
    bsz, L, _ = gla.shape
    nc = L // C
    indv = (np.arange(QK_W)[:, None] // GLA_DK == np.arange(GV_W)[None, :] // GLA_DV)
    bd = (np.arange(GV_W)[:, None] // GLA_DV == np.arange(QK_W)[None, :] // GLA_DK)
    ones = (np.arange(GV_W)[:, None] // GLA_DV == np.arange(GV_W)[None, :] // GLA_DV) / float(GLA_DV)
    const = lambda shape: pl.BlockSpec(shape, lambda b, c: (0,) * len(shape))
    BB = _row_tile(bsz, 4)
    kern = functools.partial(_gla_kernel, C=C, Cv=Cv, BB=BB)
    return pl.pallas_call(
        kern,
        grid=(bsz // BB, nc),
        in_specs=[pl.BlockSpec((BB, C, QK_W), lambda b, c: (b, c, 0)),
                  pl.BlockSpec((BB, C, QK_W), lambda b, c: (b, c, 1)),
                  pl.BlockSpec((BB, C, GV_W), lambda b, c: (b, c, 1)),
                  pl.BlockSpec((BB, C, GV_W), lambda b, c: (b, c, 2)),
                  pl.BlockSpec((BB, C, LANES), lambda b, c: (b, c, 0)),
                  const((GLA_RANK, QK_W)), const((1, QK_W)), const((1, GV_W)),
                  const((QK_W, GV_W)), const((GV_W, QK_W)), const((GV_W, GV_W)),
                  pl.BlockSpec((BB, GV_W, QK_W), lambda b, c: (b, 0, 0))],
        out_specs=[pl.BlockSpec((BB, C, GV_W), lambda b, c: (b, c, 0)),
                   pl.BlockSpec((BB, GV_W, QK_W), lambda b, c: (b, 0, 0))],
        out_shape=[jax.ShapeDtypeStruct((bsz, L, GV_W), F32),
                   jax.ShapeDtypeStruct((bsz, GV_W, QK_W), F32)],
        scratch_shapes=[pltpu.VMEM((BB, GV_W, QK_W), F32), pltpu.VMEM((BB, C, QK_W), F32),
                        pltpu.VMEM((BB, C * C, QK_W), BF16), pltpu.VMEM((BB, C * C, QK_W), BF16),
                        pltpu.VMEM((BB, C * C, GV_W), F32)],
        compiler_params=_cparams(("parallel", "arbitrary")),
        name="gla",
    )(gla, gla, gla, gla, misc, wa2, ba.reshape(1, QK_W), ng.reshape(1, GV_W),
      jnp.asarray(indv, BF16), jnp.asarray(bd, F32), jnp.asarray(ones, BF16), s0_t)


def _gla_state_to_t(s0):
    bsz = s0.shape[0]
    st = jnp.swapaxes(s0, 2, 3)
    eye = jnp.eye(GLA_HEADS, dtype=s0.dtype)
    full = st[:, :, :, None, :] * eye[None, :, None, :, None]
    return full.reshape(bsz, GV_W, QK_W)


def _gla_state_from_t(st):
    bsz = st.shape[0]
    full = st.reshape(bsz, GLA_HEADS, GLA_DV, GLA_HEADS, GLA_DK)
    diag = jnp.stack([full[:, h, :, h, :] for h in range(GLA_HEADS)], axis=1)
    return jnp.swapaxes(diag, 2, 3)


def _lru_gates(xc, wa_ref, ba_ref, wx_ref, bx_ref, sp_ref):
    r = jax.nn.sigmoid(_dot3(xc, wa_ref[...]) + ba_ref[...])
    i = jax.nn.sigmoid(_dot3(xc, wx_ref[...]) + bx_ref[...])
    log_a = -LRU_C * r * sp_ref[...]
    a = jnp.exp(log_a)
    u = jnp.sqrt(-jnp.tanh(log_a) * (a * a + 1.0)) * (i * xc)
    return a, u


def _lru_prompt_kernel(x_ref, gate_ref, cw_ref, cb_ref, wa_ref, ba_ref, wx_ref, bx_ref, sp_ref,
                       y_ref, hfin_ref, h_scr, tail_scr, a_scr, u_scr, hs_scr, *, T):
    c = pl.program_id(1)

    @pl.when(c == 0)
    def _():
        h_scr[...] = jnp.zeros_like(h_scr)
        tail_scr[...] = jnp.zeros_like(tail_scr)

    x = x_ref[...]
    xx = jnp.concatenate([tail_scr[...], x], axis=0)
    base = 8 - (CONV_W - 1)
    xc = cb_ref[...] + sum(xx[base + j:base + j + T] * cw_ref[j:j + 1, :] for j in range(CONV_W))
    tail_scr[...] = x[T - 8:T]
    a, u = _lru_gates(xc, wa_ref, ba_ref, wx_ref, bx_ref, sp_ref)
    a_scr[...] = a
    u_scr[...] = u

    def group(gi, h):
        off = pl.multiple_of(gi * 8, 8)
        ag = a_scr[pl.ds(off, 8), :]
        ug = u_scr[pl.ds(off, 8), :]
        outs = []
        for j in range(8):
            h = ag[j:j + 1] * h + ug[j:j + 1]
            outs.append(h)
        hs_scr[pl.ds(off, 8), :] = jnp.concatenate(outs, axis=0)
        return h

    h = lax.fori_loop(0, T // 8, group, h_scr[...])
    h_scr[...] = h
    hfin_ref[...] = h
    y_ref[...] = hs_scr[...] * jax.nn.gelu(gate_ref[...])


def _lru_weights(conv_w, conv_b, w_a, b_a, w_x, b_x, lam):
    def bd(w):
        eye = jnp.eye(LRU_HEADS, dtype=w.dtype)
        return (w[:, :, None, :] * eye[:, None, :, None]).reshape(LRU_WIDTH, LRU_WIDTH)
    sp = jax.nn.softplus(-lam.astype(F32)).reshape(1, LRU_WIDTH)
    r1 = lambda t: t.reshape(1, LRU_WIDTH)
    return conv_w, r1(conv_b), bd(w_a), r1(b_a), bd(w_x), r1(b_x), sp


def _lru_prompt_call(lru, weights, T):
    bsz, L, _ = lru.shape
    W = LRU_WIDTH
    const = lambda shape: pl.BlockSpec(shape, lambda b, c: (0,) * len(shape))
    kern = functools.partial(_lru_prompt_kernel, T=T)
    return pl.pallas_call(
        kern,
        grid=(bsz, L // T),
        in_specs=[pl.BlockSpec((None, T, W), lambda b, c: (b, c, 0)),
                  pl.BlockSpec((None, T, W), lambda b, c: (b, c, 1)),
                  const((CONV_W, W)), const((1, W)), const((W, W)), const((1, W)),
                  const((W, W)), const((1, W)), const((1, W))],
        out_specs=[pl.BlockSpec((None, T, W), lambda b, c: (b, c, 0)),
                   pl.BlockSpec((None, 1, W), lambda b, c: (b, 0, 0))],
        out_shape=[jax.ShapeDtypeStruct((bsz, L, W), F32), jax.ShapeDtypeStruct((bsz, 1, W), F32)],
        scratch_shapes=[pltpu.VMEM((1, W), F32), pltpu.VMEM((8, W), F32), pltpu.VMEM((T, W), F32),
                        pltpu.VMEM((T, W), F32), pltpu.VMEM((T, W), F32)],
        compiler_params=_cparams(("parallel", "arbitrary")),
        name="lru_prompt",
    )(lru, lru, *weights)


def _lru_sample_kernel(x_ref, gate_ref, buf_ref, h0_ref, cw_ref, cb_ref, wa_ref, ba_ref, wx_ref, bx_ref, sp_ref,
                       y_ref, hfin_ref, *, L):
    xx = [buf_ref[j] for j in range(CONV_W - 1)] + [x_ref[t] for t in range(L)]
    h = h0_ref[...]
    for t in range(L):
        xc = cb_ref[...] + sum(xx[t + j] * cw_ref[j:j + 1, :] for j in range(CONV_W))
        a, u = _lru_gates(xc, wa_ref, ba_ref, wx_ref, bx_ref, sp_ref)
        h = a * h + u
        y_ref[t] = h * jax.nn.gelu(gate_ref[t])
    hfin_ref[...] = h


def _lru_sample_call(x_t, gate_t, buf_t, h0, weights):
    L, bsz, W = x_t.shape
    kern = functools.partial(_lru_sample_kernel, L=L)
    return pl.pallas_call(
        kern,
        out_shape=[jax.ShapeDtypeStruct((L, bsz, W), F32), jax.ShapeDtypeStruct((bsz, W), F32)],
        compiler_params=pltpu.CompilerParams(vmem_limit_bytes=VMEM_LIMIT),
        name="lru_sample",
    )(x_t, gate_t, buf_t, h0, *weights)


def _later_and_total(tk=ATT_BLK):
    later = np.arange(tk)[:, None] > np.arange(tk)[None, :]
    return jnp.asarray(np.concatenate([later, np.ones((tk, LANES), bool)], axis=1), BF16)


def _sb_logs(z, uo, mask):
    tk = z.shape[1]
    sp = _softplus(z)
    ls = -sp if mask is None else jnp.where(mask, -sp, 0.0)
    hi, lo = _split2(ls)
    res = _dg(hi, uo) + _dg(lo, uo)
    return z - sp, res[:, :tk], res[:, tk:]


def _head_rows(q, n_rows):
    headmask = (_iota((SB_HEADS * n_rows, HW), 0) // n_rows) == (_iota((SB_HEADS * n_rows, HW), 1) // HEAD_DIM)
    qs = jnp.where(headmask, jnp.concatenate([q * (HEAD_DIM ** -0.5)] * SB_HEADS, axis=0), 0.0)
    return qs.astype(BF16), headmask


def _pick_heads(acc, headmask, n_rows):
    return sum(jnp.where(headmask[h * n_rows:(h + 1) * n_rows], acc[h * n_rows:(h + 1) * n_rows], 0.0)
               for h in range(SB_HEADS))


def _sb_prompt_kernel(q_ref, kv_ref, uo_ref, uo2_ref, o_ref, acc_scr, cs_scr, *, BB):
    i = pl.program_id(1)
    Q = ATT_BLK
    R = SB_HEADS * Q
    heads = [_head_rows(q_ref[bi], Q) for bi in range(BB)]
    headmask = heads[0][1]

    def tile(start, tk, uo_ref_, mask):
        uo = uo_ref_[...]
        for bi in range(BB):
            kv = kv_ref[bi, pl.ds(pl.multiple_of(start, Q), tk), :]
            z = _dg(heads[bi][0], kv[:, :HW], NT)
            zl, btw, tot = _sb_logs(z, uo, mask)
            cs = cs_scr[bi]
            e = jnp.exp(zl + (jnp.concatenate([cs] * (tk // LANES), axis=1) + btw))
            w = e if mask is None else jnp.where(mask, e, 0.0)
            cs_scr[bi] = cs + tot
            acc_scr[bi] += _dg(w.astype(BF16), kv[:, HW:])

    acc_scr[...] = jnp.zeros_like(acc_scr)
    cs_scr[...] = jnp.zeros_like(cs_scr)
    tile(i * Q, Q, uo_ref, _iota((R, Q), 1) < _iota((R, Q), 0) % Q)

    @pl.when(i % 2 == 1)
    def _():
        tile((i - 1) * Q, Q, uo_ref, None)

    n_pairs = i // 2

    def alive():
        cs = functools.reduce(jnp.maximum, [cs_scr[bi] for bi in range(BB)])
        return jnp.max(jnp.max(cs, axis=0, keepdims=True)) > -SB_DEAD

    def body(c):
        tile((n_pairs - 1 - c[0]) * 2 * Q, 2 * Q, uo2_ref, None)
        return c[0] + 1, alive()

    lax.while_loop(lambda c: jnp.logical_and(c[0] < n_pairs, c[1]), body, (0, alive()))
    for bi in range(BB):
        o_ref[bi] = _pick_heads(acc_scr[bi], headmask, Q)


def _sb_prompt_call(sbq, sbkv_b):
    bsz, L, _ = sbq.shape
    R = SB_HEADS * ATT_BLK
    BB = _row_tile(bsz, 2)
    return pl.pallas_call(
        functools.partial(_sb_prompt_kernel, BB=BB),
        grid=(bsz // BB, L // ATT_BLK),
        in_specs=[pl.BlockSpec((BB, ATT_BLK, HW), lambda b, i: (b, i, 0)),
                  pl.BlockSpec((BB, L, 2 * HW), lambda b, i: (b, 0, 0)),
                  pl.BlockSpec((ATT_BLK, ATT_BLK + LANES), lambda b, i: (0, 0)),
                  pl.BlockSpec((2 * ATT_BLK, 2 * ATT_BLK + LANES), lambda b, i: (0, 0))],
        out_specs=pl.BlockSpec((BB, ATT_BLK, HW), lambda b, i: (b, i, 0)),
        out_shape=jax.ShapeDtypeStruct((bsz, L, HW), F32),
        scratch_shapes=[pltpu.VMEM((BB, R, HW), F32), pltpu.VMEM((BB, R, ATT_BLK), F32)],
        compiler_params=_cparams(("parallel", "arbitrary")),
        name="sb_prompt",
    )(sbq, sbkv_b, _later_and_total(), _later_and_total(2 * ATT_BLK))


def _sb_sample_kernel(pt_ref, alive_ref, q_ref, init_a_ref, init_b_ref, *rest, L, pps, first):
    page_refs = rest[:pps]
    uo_ref, o_ref, acc_ref, cs_ref = rest[pps:]
    b = pl.program_id(0)
    s = pl.program_id(1)
    R = SB_HEADS * L
    uo = uo_ref[...]
    q16, headmask = _head_rows(q_ref[...], L)

    @pl.when(s == 0)
    def _():
        if first:
            new = init_a_ref[...]
            mask = _iota((R, ATT_BLK), 1) < (_iota((R, ATT_BLK), 0) % L)
            zl, btw, tot = _sb_logs(_dg(q16, new[:, :HW].astype(BF16), NT), uo, mask)
            w = jnp.where(mask, jnp.exp(zl + btw), 0.0)
            cs_ref[...] = tot
            acc_ref[...] = _dg(w.astype(BF16), new[:, HW:].astype(BF16))
        else:
            acc_ref[...] = init_a_ref[...]
            cs_ref[...] = init_b_ref[...]

    @pl.when(alive_ref[b] > 0)
    def _():
        z = jnp.concatenate([_dg(q16, page_refs[r][:HW, :].astype(BF16)) for r in range(pps)], axis=0)
        zl, btw, tot = _sb_logs(z, uo, None)
        cs = cs_ref[...]
        acc = acc_ref[...]
        for r in range(pps):
            sl = slice(r * R, (r + 1) * R)
            w = jnp.exp(zl[sl] + (cs + btw[sl]))
            acc = acc + _dg(w.astype(BF16), page_refs[r][HW:, :].astype(BF16), NT)
            cs = cs + tot[sl]
        acc_ref[...] = acc
        cs_ref[...] = cs

    @pl.when(s == pl.num_programs(1) - 1)
    def _():
        o_ref[...] = _pick_heads(acc_ref[...], headmask, L)


def _sb_sample_stage(sbq, init_a, init_b, alive, cache_t, page_table, layer, pps, first):
    bsz, L, _ = sbq.shape
    n_pages = page_table.shape[1]
    steps = 1 if first else n_pages // pps - 1
    skip = 0 if first else 1
    R = SB_HEADS * L
    kern = functools.partial(_sb_sample_kernel, L=L, pps=pps, first=first)

    def page_spec(r):
        return pl.BlockSpec((None, None, 2 * HW, ATT_BLK),
                            lambda b, s, pt, al: (layer, jnp.where(al[b] > 0,
                                                                   pt[b, n_pages - 1 - ((s + skip) * pps + r)], 0),
                                                  0, 0))

    per_b = lambda shape: pl.BlockSpec((None,) + shape, lambda b, s, pt, al: (b, 0, 0))
    grid_spec = pltpu.PrefetchScalarGridSpec(
        num_scalar_prefetch=2,
        grid=(bsz, steps),
        in_specs=[per_b((L, HW)), per_b(init_a.shape[1:]), per_b(init_b.shape[1:])]
                 + [page_spec(r) for r in range(pps)]
                 + [pl.BlockSpec((ATT_BLK, 2 * ATT_BLK), lambda b, s, pt, al: (0, 0))],
        out_specs=[per_b((L, HW)), per_b((R, HW)), per_b((R, ATT_BLK))],
    )
    return pl.pallas_call(
        kern,
        grid_spec=grid_spec,
        out_shape=[jax.ShapeDtypeStruct((bsz, L, HW), F32), jax.ShapeDtypeStruct((bsz, R, HW), F32),
                   jax.ShapeDtypeStruct((bsz, R, ATT_BLK), F32)],
        compiler_params=_cparams(("parallel", "arbitrary")),
        name="sb_sample",
    )(page_table, alive, sbq, init_a, init_b, *([cache_t] * pps), _later_and_total())


def _sb_sample_call(sbq, new_kv, cache_t, page_table, layer, pps):
    bsz, L, _ = sbq.shape
    R = SB_HEADS * L
    y, acc, cs = _sb_sample_stage(sbq, new_kv, jnp.zeros((bsz, R, ATT_BLK), F32), jnp.ones((bsz,), jnp.int32),
                                  cache_t, page_table, layer, pps, True)
    if page_table.shape[1] // pps > 1:
        alive = (jnp.max(cs, axis=(1, 2)) > -SB_DEAD).astype(jnp.int32)
        y = lax.cond(jnp.any(alive > 0),
                     lambda: _sb_sample_stage(sbq, acc, cs, alive, cache_t, page_table, layer, pps, False)[0],
                     lambda: y)
    return y


def _masked_softmax(s, valid):
    sm = jnp.where(valid, s, NEG)
    e = jnp.where(valid, jnp.exp(sm - jnp.max(sm, axis=1, keepdims=True)), 0.0)
    return e / jnp.maximum(jnp.sum(e, axis=1, keepdims=True), 1e-30)


def _top_select(score, n_sel):
    T, N = score.shape
    midx = _iota((N, N), 0)
    nidx = _iota((N, N), 1)
    ranks = []
    for t in range(T):
        row = score[t:t + 1, :]
        col = jnp.sum(jnp.where(midx == nidx, row, 0.0), axis=1, keepdims=True)
        beats = jnp.where(col > row, 1.0, jnp.where((col == row) & (midx < nidx), 1.0, 0.0))
        ranks.append(jnp.sum(beats, axis=0, keepdims=True))
    return jnp.where(jnp.concatenate(ranks, axis=0) < n_sel, 1.0, 0.0)


def _nsa_prompt_kernel(q_ref, rows_ref, win_ref, misc_ref, ex_ref, o_ref,
                       kcvc_scr, kse_scr, vse_scr, kwe_scr, vwe_scr, score_scr, pick_scr, s_scr, mrun_scr,
                       m_scr, acc_scr, *, L):
    i = pl.program_id(1)
    nb = L // NSA_BLOCK
    Q = NSA_Q
    G = NSA_HEADS
    R = G * Q

    @pl.when(i == 0)
    def _():
        blk = rows_ref[:, :2 * HEAD_DIM].reshape(nb, NSA_BLOCK, 2 * HEAD_DIM)
        kcvc_scr[...] = jnp.sum(blk, axis=1) * (1.0 / NSA_BLOCK)
        ksvs = rows_ref[:, 2 * HEAD_DIM:]
        kse_scr[...] = _low_half(ksvs).astype(BF16)
        vse_scr[...] = _low_half(_swap_halves(ksvs), 1.0).astype(BF16)
        kwvw = win_ref[...]
        kwe_scr[...] = _low_half(kwvw).astype(BF16)
        vwe_scr[...] = _low_half(_swap_halves(kwvw), 1.0).astype(BF16)

    q4 = (_heads_to_rows(q_ref[...]) * (HEAD_DIM ** -0.5)).astype(BF16)
    qpos1 = i * Q + _iota((Q, 1), 0)
    qpos4 = jnp.concatenate([qpos1] * G, axis=0)

    kcvc = kcvc_scr[...]
    s_c = _dg(q4, _low_half(kcvc).astype(BF16), NT)
    valid_c = (_iota((R, nb), 1) + 1) * NSA_BLOCK - 1 <= qpos4
    p_c = _masked_softmax(s_c, valid_c)
    o_c = _dg(p_c.astype(BF16), _swap_halves(kcvc).astype(BF16))

    imp = sum(p_c[h * Q:(h + 1) * Q] for h in range(G))
    blk1 = _iota((Q, nb), 1)
    score = jnp.where(blk1 == qpos1 // NSA_BLOCK, NSA_FORCE, jnp.where(blk1 * NSA_BLOCK <= qpos1, imp, -1.0))
    score_t = score.T
    score_scr[...] = score_t
    nidx = _iota((nb, Q), 0)

    def rank_body(m, rank):
        row = score_scr[pl.ds(m, 1), :]
        earlier = jnp.where(nidx > m, 1.0, 0.0)
        return rank + jnp.where(row > score_t, 1.0, jnp.where(row == score_t, earlier, 0.0))

    rank = lax.fori_loop(0, nb, rank_body, jnp.zeros((nb, Q), F32), unroll=8)
    sel_t = jnp.where(rank < min(NSA_TOPK, nb), 1.0, 0.0).astype(BF16)
    picked = _dg(sel_t, ex_ref[...], TN)
    causal = _iota((Q, L), 1) <= qpos1
    pick_scr[...] = jnp.where(causal & (picked > 0.5), 0.0, NEG)

    n_tiles = (i * Q + Q + SEL_TILE - 1) // SEL_TILE
    n_chunks = SEL_TILE // LANES
    mrun_scr[...] = jnp.full(mrun_scr.shape, NEG, F32)

    def pass1(j, c):
        off = pl.multiple_of(j * SEL_TILE, SEL_TILE)
        bias = pick_scr[:, pl.ds(off, SEL_TILE)]
        kb = kse_scr[pl.ds(off, SEL_TILE), :]
        for h in range(G):
            sl = slice(h * Q, (h + 1) * Q)
            sm = _dg(q4[sl], kb, NT) + bias
            s_scr[sl, pl.ds(off, SEL_TILE)] = sm
            mh = functools.reduce(jnp.maximum, [sm[:, k * LANES:(k + 1) * LANES] for k in range(n_chunks)])
            mrun_scr[sl, :] = jnp.maximum(mrun_scr[sl, :], mh)
        return c

    lax.fori_loop(0, n_tiles, pass1, 0)
    m_scr[...] = jnp.broadcast_to(jnp.max(mrun_scr[...], axis=1, keepdims=True), (R, LANES))
    acc_scr[...] = jnp.zeros_like(acc_scr)

    def pass2(j, c):
        off = pl.multiple_of(j * SEL_TILE, SEL_TILE)
        vb = vse_scr[pl.ds(off, SEL_TILE), :]
        for h in range(G):
            sl = slice(h * Q, (h + 1) * Q)
            p = jnp.exp(s_scr[sl, pl.ds(off, SEL_TILE)] - jnp.concatenate([m_scr[sl, :]] * n_chunks, axis=1))
            acc_scr[sl, :] += _dg(p.astype(BF16), vb)
        return c

    lax.fori_loop(0, n_tiles, pass2, 0)
    acc = acc_scr[...]
    o_s = acc / jnp.maximum(_swap_halves(acc), 1e-30)

    col = _iota((Q, ATT_BLK), 1)
    n_win = (NSA_WINDOW + Q) // ATT_BLK
    last_tile = (i + 1) * (Q // ATT_BLK) - 1
    offs, biases = [], []
    for t in range(n_win):
        j = last_tile - t
        offs.append(pl.multiple_of(jnp.maximum(j, 0) * ATT_BLK, ATT_BLK))
        kpos = j * ATT_BLK + col
        dist = qpos1 - kpos
        biases.append(jnp.where((dist >= 0) & (dist <= NSA_WINDOW) & (kpos >= 0), 0.0, NEG))
    bias_w = jnp.concatenate(biases, axis=1)
    kw = jnp.concatenate([kwe_scr[pl.ds(o, ATT_BLK), :] for o in offs], axis=0)
    vw = jnp.concatenate([vwe_scr[pl.ds(o, ATT_BLK), :] for o in offs], axis=0)
    o_w_parts = []
    for h in range(G):
        sm = _dg(q4[h * Q:(h + 1) * Q], kw, NT) + bias_w
        p = jnp.exp(sm - jnp.max(sm, axis=1, keepdims=True))
        o_w_parts.append(_dg(p.astype(BF16), vw))
    acc = jnp.concatenate(o_w_parts, axis=0)
    o_w = acc / jnp.maximum(_swap_halves(acc), 1e-30)

    gates = jax.nn.sigmoid(misc_ref[:, GLA_RANK:GLA_RANK + 3 * G])
    parts = []
    for h in range(G):
        sl = slice(h * Q, (h + 1) * Q)
        parts.append(gates[:, 3 * h:3 * h + 1] * o_c[sl] + gates[:, 3 * h + 1:3 * h + 2] * o_s[sl]
                     + gates[:, 3 * h + 2:3 * h + 3] * o_w[sl])
    o_ref[...] = _rows_to_heads(parts)


def _nsa_prompt_call(nq, rows, win, misc):
    bsz, L, _ = nq.shape
    nb = L // NSA_BLOCK
    Q = NSA_Q
    R = NSA_HEADS * Q
    expand = jnp.asarray(np.arange(nb)[:, None] == np.arange(L)[None, :] // NSA_BLOCK, BF16)
    kern = functools.partial(_nsa_prompt_kernel, L=L)
    return pl.pallas_call(
        kern,
        grid=(bsz, L // Q),
        in_specs=[pl.BlockSpec((None, Q, HW), lambda b, i: (b, i, 0)),
                  pl.BlockSpec((None, L, HW), lambda b, i: (b, 0, 0)),
                  pl.BlockSpec((None, L, 2 * HEAD_DIM), lambda b, i: (b, 0, 0)),
                  pl.BlockSpec((None, Q, LANES), lambda b, i: (b, i, 0)),
                  pl.BlockSpec((nb, L), lambda b, i: (0, 0))],
        out_specs=pl.BlockSpec((None, Q, HW), lambda b, i: (b, i, 0)),
        out_shape=jax.ShapeDtypeStruct((bsz, L, HW), F32),
        scratch_shapes=[pltpu.VMEM((nb, LANES), F32),
                        pltpu.VMEM((L, LANES), BF16), pltpu.VMEM((L, LANES), BF16),
                        pltpu.VMEM((L, LANES), BF16), pltpu.VMEM((L, LANES), BF16),
                        pltpu.VMEM((nb, Q), F32), pltpu.VMEM((Q, L), F32),
                        pltpu.VMEM((R, L), F32), pltpu.VMEM((R, LANES), F32),
                        pltpu.VMEM((R, LANES), F32), pltpu.VMEM((R, LANES), F32)],
        compiler_params=_cparams(("parallel", "arbitrary")),
        name="nsa_prompt",
    )(nq, rows, win, misc, expand)


def _nsa_sample_kernel(pt_ref, q_ref, new_ref, winp_ref, winn_ref, misc_ref, ea_ref, eb_ref, *rest,
                       L, pps, n_pages):
    page_refs = rest[:pps]
    o_ref, kcvc_scr, ksvs_scr = rest[pps:]
    s = pl.program_id(1)
    steps = n_pages // pps
    G = NSA_HEADS
    R = G * L
    per_page = ATT_BLK // NSA_BLOCK
    bps = pps * per_page
    gk = pps * ATT_BLK
    nbp = n_pages * per_page
    nbp_pad = kcvc_scr.shape[1]
    q4 = (_heads_to_rows(q_ref[...]) * (HEAD_DIM ** -0.5)).astype(BF16)
    t_row = _iota((R, 1), 0) % L

    @pl.when(s == 0)
    def _():
        kcvc_scr[...] = jnp.zeros_like(kcvc_scr)

    x = jnp.concatenate([page_refs[r][:2 * HEAD_DIM, :] for r in range(pps)], axis=1)
    means = _dot_xl(x, ea_ref[...])
    base = (s * bps) % LANES
    off = pl.multiple_of(((s * bps) // LANES) * LANES, LANES)
    kcvc_scr[:, pl.ds(off, LANES)] += pltpu.roll(means, base, 1)
    for r in range(pps):
        koff = pl.multiple_of((s * pps + r) * ATT_BLK, ATT_BLK)
        ksvs_scr[:, pl.ds(koff, ATT_BLK)] = page_refs[r][2 * HEAD_DIM:, :].astype(BF16)

    @pl.when(s == steps - 1)
    def _():
        kcvc = kcvc_scr[...].astype(BF16)
        live = _iota((R, nbp_pad), 1) < nbp
        s_c = _dg(q4, kcvc)
        p_c = _masked_softmax(s_c, live)
        o_c = _swap_halves(_dg(p_c.astype(BF16), kcvc, NT))
        imp = sum(p_c[h * L:(h + 1) * L] for h in range(G))
        imp = jnp.where(live[:L], imp, -3.0)
        sel = _top_select(imp, min(NSA_TOPK, nbp + 1) - 1)
        sel = jnp.concatenate([sel] * G, axis=0).astype(BF16)
        n_buf = winp_ref.shape[1]
        winp = winp_ref[...].astype(BF16)
        winn = winn_ref[...]
        dist_p = t_row + n_buf - _iota((R, n_buf), 1)
        dist_n = t_row - _iota((R, ATT_BLK), 1)
        s_w = jnp.concatenate([_dg(q4, winp), _dg(q4, _low_half(winn).astype(BF16), NT)], axis=1)
        valid_w = jnp.concatenate([(dist_p >= 0) & (dist_p <= NSA_WINDOW),
                                   (dist_n >= 0) & (dist_n <= NSA_WINDOW)], axis=1)
        p_w = _masked_softmax(s_w, valid_w).astype(BF16)
        o_w = _swap_halves(_dg(p_w[:, :n_buf], winp, NT) + _dg(p_w[:, n_buf:], winn.astype(BF16)))
        ksvs_new = new_ref[:, 2 * HEAD_DIM:]
        sms = [jnp.where(dist_n >= 0, _dg(q4, _low_half(ksvs_new).astype(BF16), NT), NEG)]
        for g in range(steps):
            picked = _dg(sel[:, g * bps:(g + 1) * bps], eb_ref[...]) > 0.5
            sms.append(jnp.where(picked, _dg(q4, ksvs_scr[:, g * gk:(g + 1) * gk]), NEG))
        m = functools.reduce(jnp.maximum, [jnp.max(t, axis=1, keepdims=True) for t in sms])
        p0 = jnp.exp(sms[0] - m)
        l = jnp.sum(p0, axis=1, keepdims=True)
        acc = _dg(p0.astype(BF16), ksvs_new.astype(BF16))
        for g in range(steps):
            p = jnp.exp(sms[g + 1] - m)
            l = l + jnp.sum(p, axis=1, keepdims=True)
            acc = acc + _dg(p.astype(BF16), ksvs_scr[:, g * gk:(g + 1) * gk], NT)
        o_s = _swap_halves(acc) / jnp.maximum(l, 1e-30)
        gates = jax.nn.sigmoid(misc_ref[:, GLA_RANK:GLA_RANK + 3 * G])
        parts = []
        for h in range(G):
            sl = slice(h * L, (h + 1) * L)
            parts.append(gates[:, 3 * h:3 * h + 1] * o_c[sl] + gates[:, 3 * h + 1:3 * h + 2] * o_s[sl]
                         + gates[:, 3 * h + 2:3 * h + 3] * o_w[sl])
        o_ref[...] = _rows_to_heads(parts)


def _nsa_sample_call(nq, new_rows, win_past_t, win_new, misc, cache_t, page_table, layer, pps):
    bsz, L, _ = nq.shape
    n_pages = page_table.shape[1]
    steps = n_pages // pps
    n_buf = win_past_t.shape[3]
    per_page = ATT_BLK // NSA_BLOCK
    bps = pps * per_page
    nbp = n_pages * per_page
    nbp_pad = -(-nbp // LANES) * LANES
    assert LANES % bps == 0
    R = NSA_HEADS * L
    keys = np.arange(pps * ATT_BLK)
    ea = jnp.asarray((keys[:, None] // NSA_BLOCK == np.arange(LANES)[None, :]) / float(NSA_BLOCK), BF16)
    eb = jnp.asarray(np.arange(bps)[:, None] == keys[None, :] // NSA_BLOCK, BF16)
    kern = functools.partial(_nsa_sample_kernel, L=L, pps=pps, n_pages=n_pages)

    def page_spec(r):
        return pl.BlockSpec((None, None, HW, ATT_BLK), lambda b, s, pt: (layer, pt[b, s * pps + r], 0, 0))

    cst = lambda shape: pl.BlockSpec(shape, lambda b, s, pt: (0,) * len(shape))
    grid_spec = pltpu.PrefetchScalarGridSpec(
        num_scalar_prefetch=1,
        grid=(bsz, steps),
        in_specs=[pl.BlockSpec((None, L, HW), lambda b, s, pt: (b, 0, 0)),
                  pl.BlockSpec((None, ATT_BLK, HW), lambda b, s, pt: (b, 0, 0)),
                  pl.BlockSpec((None, None, 2 * HEAD_DIM, n_buf), lambda b, s, pt: (layer, b, 0, 0)),
                  pl.BlockSpec((None, ATT_BLK, 2 * HEAD_DIM), lambda b, s, pt: (b, 0, 0)),
                  pl.BlockSpec((None, L, LANES), lambda b, s, pt: (b, 0, 0)),
                  cst((pps * ATT_BLK, LANES)), cst((bps, pps * ATT_BLK))]
                 + [page_spec(r) for r in range(pps)],
        out_specs=pl.BlockSpec((None, L, HW), lambda b, s, pt: (b, 0, 0)),
        scratch_shapes=[pltpu.VMEM((2 * HEAD_DIM, nbp_pad), F32),
                        pltpu.VMEM((2 * HEAD_DIM, n_pages * ATT_BLK), BF16)],
    )
    return pl.pallas_call(
        kern,
        grid_spec=grid_spec,
        out_shape=jax.ShapeDtypeStruct((bsz, L, HW), F32),
        compiler_params=_cparams(("parallel", "arbitrary")),
        name="nsa_sample",
    )(page_table, nq, new_rows, win_past_t, win_new, misc, ea, eb, *([cache_t] * pps))


def _outproj_kernel(ya_ref, yb_ref, yc_ref, yd_ref, x_ref, w_ref, g_ref, b_ref, o_ref):
    y = jnp.concatenate([ya_ref[...], yb_ref[...], yc_ref[...], yd_ref[...]], axis=1).astype(BF16)
    h = jnp.dot(y, w_ref[...], preferred_element_type=F32)
    o_ref[...] = _layer_norm(DN_ALPHA * x_ref[...] + h, g_ref[...], b_ref[...])


def _outproj_call(ya, yb, yc, yd, x, w, g, b, tm):
    n = x.shape[0]
    part = lambda: pl.BlockSpec((tm, HW), lambda i: (i, 0))
    return pl.pallas_call(
        _outproj_kernel,
        grid=(n // tm,),
        in_specs=[part(), part(), part(), part(),
                  pl.BlockSpec((tm, D_MODEL), lambda i: (i, 0)),
                  pl.BlockSpec((4 * HW, D_MODEL), lambda i: (0, 0)),
                  pl.BlockSpec((1, D_MODEL), lambda i: (0, 0)),
                  pl.BlockSpec((1, D_MODEL), lambda i: (0, 0))],
        out_specs=pl.BlockSpec((tm, D_MODEL), lambda i: (i, 0)),
        out_shape=jax.ShapeDtypeStruct((n, D_MODEL), F32),
        compiler_params=_cparams(("parallel",)),
        name="out_proj_ln",
    )(ya, yb, yc, yd, x, w, g.reshape(1, D_MODEL), b.reshape(1, D_MODEL))


def _route(logits):
    lane = _iota(logits.shape, 1)
    live = lane < N_EXPERTS
    lg = jnp.where(live, logits, NEG)
    e = jnp.where(live, jnp.exp(lg - jnp.max(lg, axis=1, keepdims=True)), 0.0)
    probs = e / jnp.sum(e, axis=1, keepdims=True)
    big = logits.shape[1]

    def top2(vals):
        w1 = jnp.max(vals, axis=1, keepdims=True)
        i1 = jnp.min(jnp.where(vals == w1, lane, big), axis=1, keepdims=True)
        rest = jnp.where(lane == i1, -2.0, vals)
        w2 = jnp.max(rest, axis=1, keepdims=True)
        i2 = jnp.min(jnp.where(rest == w2, lane, big), axis=1, keepdims=True)
        return w1, i1, w2, i2

    best = None
    g_sel = None
    for g in range(N_GROUPS):
        in_g = (lane // EXPERTS_PER_GROUP) == g
        w1, _, w2, _ = top2(jnp.where(in_g, probs, -1.0))
        tot = w1 + w2
        if best is None:
            best, g_sel = tot, jnp.zeros_like(tot, dtype=jnp.int32)
        else:
            upd = tot > best
            g_sel = jnp.where(upd, g, g_sel)
            best = jnp.where(upd, tot, best)
    in_grp = live & ((lane // EXPERTS_PER_GROUP) == g_sel)
    w1, i1, w2, i2 = top2(jnp.where(in_grp, probs, -1.0))
    den = w1 + w2
    return jnp.where(lane == i1, w1 / den, 0.0) + jnp.where(lane == i2, w2 / den, 0.0)


def _moe_kernel(x_ref, rw_ref, wg_ref, wu_ref, wd_ref, g_ref, b_ref, o_ref, acc_scr, comb_scr, xb_scr):
    e = pl.program_id(1)

    @pl.when(e == 0)
    def _():
        x = x_ref[...]
        xb_scr[...] = x.astype(BF16)
        comb_scr[...] = _route(_dot3(x, rw_ref[...]))
        acc_scr[...] = jnp.zeros_like(acc_scr)

    xb = xb_scr[...]
    hg = jnp.dot(xb, wg_ref[...], preferred_element_type=F32)
    hu = jnp.dot(xb, wu_ref[...], preferred_element_type=F32)
    comb = comb_scr[...]
    c = jnp.sum(jnp.where(_iota(comb.shape, 1) == e, comb, 0.0), axis=1, keepdims=True)
    h = (hg * jax.nn.sigmoid(hg)) * hu * c
    acc_scr[...] += jnp.dot(h.astype(BF16), wd_ref[...], preferred_element_type=F32)

    @pl.when(e == pl.num_programs(1) - 1)
    def _():
        o_ref[...] = _layer_norm(DN_ALPHA * x_ref[...] + acc_scr[...], g_ref[...], b_ref[...])


def _moe_call(x, rw, wg, wu, wd, g, b, tm):
    n = x.shape[0]
    return pl.pallas_call(
        _moe_kernel,
        grid=(n // tm, N_EXPERTS),
        in_specs=[pl.BlockSpec((tm, D_MODEL), lambda i, e: (i, 0)),
                  pl.BlockSpec((D_MODEL, LANES), lambda i, e: (0, 0)),
                  pl.BlockSpec((None, D_MODEL, D_EXPERT), lambda i, e: (e, 0, 0)),
                  pl.BlockSpec((None, D_MODEL, D_EXPERT), lambda i, e: (e, 0, 0)),
                  pl.BlockSpec((None, D_EXPERT, D_MODEL), lambda i, e: (e, 0, 0)),
                  pl.BlockSpec((1, D_MODEL), lambda i, e: (0, 0)),
                  pl.BlockSpec((1, D_MODEL), lambda i, e: (0, 0))],
        out_specs=pl.BlockSpec((tm, D_MODEL), lambda i, e: (i, 0)),
        out_shape=jax.ShapeDtypeStruct((n, D_MODEL), F32),
        scratch_shapes=[pltpu.VMEM((tm, D_MODEL), F32), pltpu.VMEM((tm, LANES), F32),
                        pltpu.VMEM((tm, D_MODEL), BF16)],
        compiler_params=_cparams(("parallel", "arbitrary")),
        name="moe_ln",
    )(x, rw, wg, wu, wd, g.reshape(1, D_MODEL), b.reshape(1, D_MODEL))


def _pad_rows(t, n):
    return jnp.pad(t, ((0, 0), (0, n - t.shape[1]), (0, 0)))


def _row_tile(n, pref):
    t = min(pref, n)
    while n % t:
        t //= 2
    return t


def _keys_minor(cache):
    d, p, rows = cache.shape[:3]
    return jnp.transpose(cache, (0, 1, 3, 4, 5, 2)).reshape(d, p, -1, rows)


def kernel(x_prompt, x_sample, cache_sb_kv, cache_nsa_kv, cache_nsa_win, state_gla, state_lru_h, state_lru_conv, page_table, ln_in_g, ln_in_b, w_in, gla_w_a2, gla_b_a, gla_norm_g, lru_conv_w, lru_conv_b, lru_w_a, lru_b_a, lru_w_x, lru_b_x, lru_lambda, w_out, ln1_g, ln1_b, router_w, moe_w_gate, moe_w_up, moe_w_down, ln2_g, ln2_b):
    bp, lp, d = x_prompt.shape
    bs, ls, _ = x_sample.shape
    depth = w_in.shape[0]
    page = cache_sb_kv.shape[2]
    n_pages = page_table.shape[1]
    past_len = n_pages * page
    n_buf = cache_nsa_win.shape[2]
    n_p, n_s = bp * lp, bs * ls
    assert page == ATT_BLK and lp % SEL_TILE == 0 and ls <= 8 and d == D_MODEL
    assert n_pages * (ATT_BLK // NSA_BLOCK) >= NSA_TOPK - 1 and n_buf <= past_len

    tm_p = _row_tile(n_p, 512)
    tm_s = _row_tile(n_s, 512)
    tab_p = _rope_table(jnp.arange(lp))
    tab_s = jnp.tile(_rope_table(past_len + jnp.arange(ls)), (bs, 1))
    assert lp % tm_p == 0 and tm_s == n_s

    cache_sb_t = _keys_minor(cache_sb_kv)
    cache_nsa_t = _keys_minor(cache_nsa_kv)
    win_past_t = _keys_minor(cache_nsa_win)
    win_past = cache_nsa_win.reshape(depth, bs, n_buf, 2 * HEAD_DIM)
    rw = jnp.pad(router_w, ((0, 0), (0, LANES - N_EXPERTS)))
    pps = 16
    while n_pages % pps:
        pps //= 2
    gla_c = 32
    lru_t = _row_tile(lp, 512)

    xp = _ln_call(x_prompt.reshape(n_p, d), ln_in_g, ln_in_b, tm_p)
    xs = _ln_call(x_sample.reshape(n_s, d), ln_in_g, ln_in_b, tm_s)

    st_p, st_s = [], []
    for l in range(depth):
        w_l = _prep_w_in(w_in[l])
        lru_w = _lru_weights(lru_conv_w[l], lru_conv_b[l], lru_w_a[l], lru_b_a[l], lru_w_x[l], lru_b_x[l],
                             lru_lambda[l])
        w_out_l = w_out[l].astype(BF16)
        wg, wu, wd = moe_w_gate[l].astype(BF16), moe_w_up[l].astype(BF16), moe_w_down[l].astype(BF16)

        gla, lru, sbq, sbkv, nq, nsa, win, misc, sbkv_b = _proj_call(xp, w_l, tab_p, tm_p)
        r3 = lambda t, b_, l_: t.reshape(b_, l_, t.shape[-1])
        gla3, lru3, sbq3, sbkv3 = r3(gla, bp, lp), r3(lru, bp, lp), r3(sbq, bp, lp), r3(sbkv, bp, lp)
        nq3, nsa3, win3, misc3 = r3(nq, bp, lp), r3(nsa, bp, lp), r3(win, bp, lp), r3(misc, bp, lp)
        y_a, gla_t = _gla_call(gla3, misc3, gla_w_a2[l], gla_b_a[l], gla_norm_g[l],
                               jnp.zeros((bp, GV_W, QK_W), F32), gla_c, gla_c)
        y_b, lru_h = _lru_prompt_call(lru3, lru_w, lru_t)
        y_c = _sb_prompt_call(sbq3, sbkv_b.reshape(bp, lp, 2 * HW))
        y_d = _nsa_prompt_call(nq3, nsa3, win3, misc3)
        x1 = _outproj_call(y_a.reshape(n_p, HW), y_b.reshape(n_p, HW), y_c.reshape(n_p, HW), y_d.reshape(n_p, HW),
                           xp, w_out_l, ln1_g[l], ln1_b[l], tm_p)
        xp = _moe_call(x1, rw, wg, wu, wd, ln2_g[l], ln2_b[l], _row_tile(n_p, 1024))
        wn = min(NSA_WINDOW, lp)
        st_p.append((sbkv3.reshape(bp, lp, 2, SB_HEADS, HEAD_DIM), nsa3.reshape(bp, lp, 4, 1, HEAD_DIM),
                     win3[:, lp - wn:].reshape(bp, wn, 2, 1, HEAD_DIM), _gla_state_from_t(gla_t),
                     lru_h.reshape(bp, LRU_WIDTH), lru3[:, lp - (CONV_W - 1):, :LRU_WIDTH]))

        gla, lru, sbq, sbkv, nq, nsa, win, misc, _ = _proj_call(xs, w_l, tab_s, tm_s)
        gla3, lru3, sbq3, sbkv3 = r3(gla, bs, ls), r3(lru, bs, ls), r3(sbq, bs, ls), r3(sbkv, bs, ls)
        nq3, nsa3, win3, misc3 = r3(nq, bs, ls), r3(nsa, bs, ls), r3(win, bs, ls), r3(misc, bs, ls)
        gc = 16
        y_a, gla_t = _gla_call(_pad_rows(gla3, gc), _pad_rows(misc3, gc), gla_w_a2[l], gla_b_a[l], gla_norm_g[l],
                               _gla_state_to_t(state_gla[l].astype(F32)), gc, ls)
        y_a = y_a[:, :ls]
        tmaj = lambda t: jnp.swapaxes(t, 0, 1)
        y_b, lru_h = _lru_sample_call(tmaj(lru3[:, :, :LRU_WIDTH]), tmaj(lru3[:, :, LRU_WIDTH:]),
                                      tmaj(state_lru_conv[l]), state_lru_h[l].astype(F32), lru_w)
        y_b = tmaj(y_b)
        y_c = _sb_sample_call(sbq3, _pad_rows(sbkv3, ATT_BLK), cache_sb_t, page_table, l, pps)
        y_d = _nsa_sample_call(nq3, _pad_rows(nsa3, ATT_BLK), win_past_t, _pad_rows(win3, ATT_BLK), misc3,
                               cache_nsa_t, page_table, l, pps)
        x1 = _outproj_call(y_a.reshape(n_s, HW), y_b.reshape(n_s, HW), y_c.reshape(n_s, HW), y_d.reshape(n_s, HW),
                           xs, w_out_l, ln1_g[l], ln1_b[l], tm_s)
        xs = _moe_call(x1, rw, wg, wu, wd, ln2_g[l], ln2_b[l], tm_s)
        win_all = jnp.concatenate([win_past[l], win3], axis=1)
        lru_buf = jnp.concatenate([state_lru_conv[l], lru3[:, :, :LRU_WIDTH]], axis=1)[:, ls:]
        st_s.append((sbkv3.reshape(bs, ls, 2, SB_HEADS, HEAD_DIM), nsa3.reshape(bs, ls, 4, 1, HEAD_DIM),
                     win_all[:, win_all.shape[1] - n_buf:].reshape(bs, n_buf, 2, 1, HEAD_DIM),
                     _gla_state_from_t(gla_t), lru_h, lru_buf))

    outs_p = [jnp.stack([st[i] for st in st_p]) for i in range(6)]
    outs_s = [jnp.stack([st[i] for st in st_s]) for i in range(6)]
    res = [xp.reshape(bp, lp, d), xs.reshape(bs, ls, d)]
    for a, b in zip(outs_p, outs_s):
        res += [a, b]
    return tuple(res)
```

```python
import functools

import jax
import jax.numpy as jnp
import numpy as np
from jax import lax
from jax.experimental import pallas as pl
from jax.experimental.pallas import tpu as pltpu

F32 = jnp.float32
BF16 = jnp.bfloat16

D_MODEL = 1024
HEAD_DIM = 64
GLA_HEADS, GLA_DK, GLA_DV, GLA_RANK, GLA_TAU = 4, 32, 64, 16, 16.0
LRU_WIDTH, LRU_HEADS, CONV_W, LRU_C = 256, 4, 4, 8.0
LRU_BLOCK = LRU_WIDTH // LRU_HEADS
SB_HEADS = 4
NSA_HEADS, NSA_BLOCK, NSA_TOPK, NSA_WINDOW, NSA_FORCE = 4, 64, 16, 512, 1.0e4
ROPE_THETA, ROPE_DIM = 500000.0, HEAD_DIM // 4
N_EXPERTS, N_GROUPS, TOP_K, D_EXPERT = 16, 4, 2, 512
EXPERTS_PER_GROUP = N_EXPERTS // N_GROUPS
DEPTH = 2
DN_ALPHA = (2.0 * DEPTH) ** 0.25
EPS = 1e-5
NEG = -1e30

QK_W = GLA_HEADS * GLA_DK
GV_W = GLA_HEADS * GLA_DV
HW = 4 * HEAD_DIM
IN_SIZES = (QK_W, QK_W, GV_W, GV_W, GLA_RANK, LRU_WIDTH, LRU_WIDTH, HW, HW, HW, HW) + (HEAD_DIM,) * 6 + (12,)
IN_OFFSETS = tuple(int(v) for v in np.cumsum(IN_SIZES)[:-1])
P_GLA, P_LRU, P_SBQ, P_SBKV, P_NQ, P_NSA, P_WIN, P_MISC = 0, 768, 1280, 1536, 2048, 2304, 2560, 2688
P_TOTAL = 2816
LANES = 128
ATT_BLK = 128
SEL_TILE = 4 * ATT_BLK
NSA_Q = 2 * ATT_BLK
VMEM_LIMIT = 56 * 1024 * 1024
SB_DEAD = 105.0

NN = (((1,), (0,)), ((), ()))
NT = (((1,), (1,)), ((), ()))
TN = (((0,), (0,)), ((), ()))


def _cparams(sem):
    return pltpu.CompilerParams(dimension_semantics=sem, vmem_limit_bytes=VMEM_LIMIT)


def _split2(a):
    hi = a.astype(BF16)
    lo = (a - hi.astype(F32)).astype(BF16)
    return hi, lo


def _split3(a):
    hi = a.astype(BF16)
    r = a - hi.astype(F32)
    mid = r.astype(BF16)
    lo = (r - mid.astype(F32)).astype(BF16)
    return hi, mid, lo


def _dg(a, b, dims=NN):
    return lax.dot_general(a, b, dims, preferred_element_type=F32)


def _dot1(a, b, dims=NN):
    return _dg(a.astype(BF16), b.astype(BF16), dims)


def _dot3(a, b, dims=NN):
    ah, al = _split2(a)
    bh, bl = _split2(b)
    return _dg(ah, bh, dims) + (_dg(ah, bl, dims) + _dg(al, bh, dims))


def _dot_xl(a, b_exact, dims=NN):
    h, m, l = _split3(a)
    return _dg(h, b_exact, dims) + (_dg(m, b_exact, dims) + _dg(l, b_exact, dims))


def _dot_lx(a_exact, b, dims=NN):
    h, m, l = _split3(b)
    return _dg(a_exact, h, dims) + (_dg(a_exact, m, dims) + _dg(a_exact, l, dims))


def _softplus(x):
    return jnp.maximum(x, 0.0) + jnp.log(1.0 + jnp.exp(-jnp.abs(x)))


def _log_sigmoid(x):
    return -_softplus(-x)


def _layer_norm(x, g, b):
    mu = jnp.mean(x, axis=-1, keepdims=True)
    xc = x - mu
    var = jnp.mean(xc * xc, axis=-1, keepdims=True)
    return xc * lax.rsqrt(var + EPS) * g + b


def _iota(shape, dim):
    return lax.broadcasted_iota(jnp.int32, shape, dim)


def _low_half(x, fill=0.0):
    return jnp.where(_iota(x.shape, 1) < HEAD_DIM, x, fill)


def _swap_halves(x):
    return pltpu.roll(x, HEAD_DIM, 1)


def _heads_to_rows(q):
    pieces = []
    for h in range(HW // HEAD_DIM):
        pair = q[:, (h // 2) * LANES:(h // 2 + 1) * LANES]
        pieces.append(_low_half(pair if h % 2 == 0 else _swap_halves(pair)))
    return jnp.concatenate(pieces, axis=0)


def _rows_to_heads(parts):
    pairs = [jnp.where(_iota(parts[0].shape, 1) < HEAD_DIM, parts[2 * p], _swap_halves(parts[2 * p + 1]))
             for p in range(len(parts) // 2)]
    return jnp.concatenate(pairs, axis=1)


def _ln_kernel(x_ref, g_ref, b_ref, o_ref):
    o_ref[...] = _layer_norm(x_ref[...], g_ref[...], b_ref[...])


def _ln_call(x, g, b, tm):
    n, d = x.shape
    return pl.pallas_call(
        _ln_kernel,
        grid=(n // tm,),
        in_specs=[pl.BlockSpec((tm, d), lambda i: (i, 0)),
                  pl.BlockSpec((1, d), lambda i: (0, 0)),
                  pl.BlockSpec((1, d), lambda i: (0, 0))],
        out_specs=pl.BlockSpec((tm, d), lambda i: (i, 0)),
        out_shape=jax.ShapeDtypeStruct((n, d), F32),
        compiler_params=_cparams(("parallel",)),
        name="ln_in",
    )(x, g.reshape(1, d), b.reshape(1, d))


def _rope(v, c, s, flags):
    ones = jnp.ones_like(c)
    zeros = jnp.zeros_like(s)
    cc = jnp.concatenate([c if f else ones for f in flags], axis=-1)
    ss = jnp.concatenate([s if f else zeros for f in flags], axis=-1)
    n = v.shape[-1]
    half = ROPE_DIM // 2
    lane = _iota(v.shape, 1) % HEAD_DIM
    sw = jnp.where(lane < half, pltpu.roll(v, n - half, 1), pltpu.roll(v, half, 1))
    return v * cc + sw * ss


def _proj_kernel(x_ref, w_ref, tab_ref, gla_ref, lru_ref, sbq_ref, sbkv_ref, nq_ref, nsa_ref, win_ref, misc_ref,
                 sbkvb_ref):
    x = x_ref[...].astype(BF16)

    def mm(off, width):
        return jnp.dot(x, w_ref[:, off:off + width], preferred_element_type=F32)

    gla_ref[...] = mm(P_GLA, 768)
    lru_ref[...] = mm(P_LRU, 512)
    sbq_ref[...] = mm(P_SBQ, 256)
    sbkv = mm(P_SBKV, 512)
    sbkv_ref[...] = sbkv
    sbkvb_ref[...] = sbkv.astype(BF16)
    tab = tab_ref[...]
    c = tab[:, :HEAD_DIM]
    s = tab[:, HEAD_DIM:]
    nq_ref[...] = _rope(mm(P_NQ, 256), c, s, (1, 1, 1, 1))
    nsa_ref[...] = _rope(mm(P_NSA, 256), c, s, (1, 0, 1, 0))
    win_ref[...] = _rope(mm(P_WIN, 128), c, s, (1, 0))
    misc_ref[...] = mm(P_MISC, 128)


def _proj_call(x, w, tab, tm):
    n = x.shape[0]
    tab_blocks = tab.shape[0] // tm
    widths = (768, 512, 256, 512, 256, 256, 128, 128)
    return pl.pallas_call(
        _proj_kernel,
        grid=(n // tm,),
        in_specs=[pl.BlockSpec((tm, D_MODEL), lambda i: (i, 0)),
                  pl.BlockSpec((D_MODEL, P_TOTAL), lambda i: (0, 0)),
                  pl.BlockSpec((tm, LANES), lambda i: (i % tab_blocks, 0))],
        out_specs=[pl.BlockSpec((tm, wd), lambda i: (i, 0)) for wd in widths + (512,)],
        out_shape=[jax.ShapeDtypeStruct((n, wd), F32) for wd in widths]
                  + [jax.ShapeDtypeStruct((n, 512), BF16)],
        compiler_params=_cparams(("parallel",)),
        name="in_proj",
    )(x, w, tab)


def _rope_table(pos):
    inv = ROPE_THETA ** (-jnp.arange(0, ROPE_DIM, 2, dtype=F32) / ROPE_DIM)
    ang = pos.astype(F32)[:, None] * inv[None, :]
    cos, sin = jnp.cos(ang), jnp.sin(ang)
    n = pos.shape[0]
    c = jnp.concatenate([cos, cos, jnp.ones((n, HEAD_DIM - ROPE_DIM), F32)], axis=-1)
    s = jnp.concatenate([-sin, sin, jnp.zeros((n, HEAD_DIM - ROPE_DIM), F32)], axis=-1)
    return jnp.concatenate([c, s], axis=-1)


def _prep_w_in(w):
    parts = jnp.split(w, IN_OFFSETS, axis=-1)
    (gq, gk, gv, gg, ga, lx, lg, sq, sk, sv, nq, nkc, nvc, nks, nvs, nkw, nvw, ngate) = parts
    pad = jnp.zeros((w.shape[0], LANES - GLA_RANK - 12), w.dtype)
    return jnp.concatenate([gq, gk, gv, gg, lx, lg, sq, sk, sv, nq, nkc, nvc, nks, nvs, nkw, nvw,
                            ga, ngate, pad], axis=-1).astype(BF16)


def _gla_kernel(q_ref, k_ref, v_ref, g_ref, misc_ref, wa2_ref, ba_ref, ng_ref, indv_ref, bd_ref, ones_ref,
                s0_ref, y_ref, st_ref, st_scr, b_scr, phi_scr, plo_scr, att_scr, *, C, Cv, BB):
    c = pl.program_id(1)

    @pl.when(c == 0)
    def _():
        st_scr[...] = s0_ref[...]

    rows = _iota((C, QK_W), 0)
    tril = (_iota((C, C), 0) >= _iota((C, C), 1)).astype(BF16)
    qs, bs = [], []
    for bi in range(BB):
        la = _log_sigmoid(_dot3(misc_ref[bi, :, :GLA_RANK], wa2_ref[...]) + ba_ref[...]) * (1.0 / GLA_TAU)
        if Cv < C:
            la = jnp.where(rows < Cv, la, 0.0)
        b = _dot_lx(tril, la)
        b_scr[bi] = b
        bs.append(b)
        qs.append(q_ref[bi] * (GLA_DK ** -0.5))

    def fill(s, carry):
        off = pl.multiple_of(s * C, C)
        for bi in range(BB):
            b_s = b_scr[bi, pl.ds(s, 1), :]
            k_s = k_ref[bi, pl.ds(s, 1), :]
            d = jnp.where(rows >= s, bs[bi] - b_s, -jnp.inf)
            hi, lo = _split2(qs[bi] * k_s * jnp.exp(d))
            phi_scr[bi, pl.ds(off, C), :] = hi
            plo_scr[bi, pl.ds(off, C), :] = lo
        return carry

    lax.fori_loop(0, Cv, fill, 0, unroll=2)
    indv = indv_ref[...]
    for bi in range(BB):
        att_scr[bi, :Cv * C, :] = _dg(phi_scr[bi, :Cv * C, :], indv) + _dg(plo_scr[bi, :Cv * C, :], indv)

    def gather(s, os):
        off = pl.multiple_of(s * C, C)
        return tuple(os[bi] + att_scr[bi, pl.ds(off, C), :] * v_ref[bi, pl.ds(s, 1), :] for bi in range(BB))

    os = lax.fori_loop(0, Cv, gather, tuple(jnp.zeros((C, GV_W), F32) for _ in range(BB)), unroll=2)
    for bi in range(BB):
        b = bs[bi]
        st = st_scr[bi]
        o = os[bi] + _dot3(qs[bi] * jnp.exp(b), st, NT)
        bl = b[C - 1:C, :]
        kd = k_ref[bi] * jnp.exp(bl - b)
        upd = _dot3(v_ref[bi], kd, TN) * bd_ref[...]
        st_new = st * jnp.exp(bl) + upd
        st_scr[bi] = st_new
        st_ref[bi] = st_new
        ms = _dot_xl(o * o, ones_ref[...])
        on = o * lax.rsqrt(ms + EPS) * ng_ref[...]
        g = g_ref[bi]
        y_ref[bi] = on * (g * jax.nn.sigmoid(g))


def _gla_call(gla, misc, wa2, ba, ng, s0_t, C, Cv):
    bsz, L, _ = gla.shape
    nc = L // C
    indv = (np.arange(QK_W)[:, None] // GLA_DK == np.arange(GV_W)[None, :] // GLA_DV)
    bd = (np.arange(GV_W)[:, None] // GLA_DV == np.arange(QK_W)[None, :] // GLA_DK)
    ones = (np.arange(GV_W)[:, None] // GLA_DV == np.arange(GV_W)[None, :] // GLA_DV) / float(GLA_DV)
    const = lambda shape: pl.BlockSpec(shape, lambda b, c: (0,) * len(shape))
    BB = _row_tile(bsz, 4)
    kern = functools.partial(_gla_kernel, C=C, Cv=Cv, BB=BB)
    return pl.pallas_call(
        kern,
        grid=(bsz // BB, nc),
        in_specs=[pl.BlockSpec((BB, C, QK_W), lambda b, c: (b, c, 0)),
                  pl.BlockSpec((BB, C, QK_W), lambda b, c: (b, c, 1)),
                  pl.BlockSpec((BB, C, GV_W), lambda b, c: (b, c, 1)),
                  pl.BlockSpec((BB, C, GV_W), lambda b, c: (b, c, 2)),
                  pl.BlockSpec((BB, C, LANES), lambda b, c: (b, c, 0)),
                  const((GLA_RANK, QK_W)), const((1, QK_W)), const((1, GV_W)),
                  const((QK_W, GV_W)), const((GV_W, QK_W)), const((GV_W, GV_W)),
                  pl.BlockSpec((BB, GV_W, QK_W), lambda b, c: (b, 0, 0))],
        out_specs=[pl.BlockSpec((BB, C, GV_W), lambda b, c: (b, c, 0)),
                   pl.BlockSpec((BB, GV_W, QK_W), lambda b, c: (b, 0, 0))],
        out_shape=[jax.ShapeDtypeStruct((bsz, L, GV_W), F32),
                   jax.ShapeDtypeStruct((bsz, GV_W, QK_W), F32)],
        scratch_shapes=[pltpu.VMEM((BB, GV_W, QK_W), F32), pltpu.VMEM((BB, C, QK_W), F32),
                        pltpu.VMEM((BB, C * C, QK_W), BF16), pltpu.VMEM((BB, C * C, QK_W), BF16),
                        pltpu.VMEM((BB, C * C, GV_W), F32)],
        compiler_params=_cparams(("parallel", "arbitrary")),
        name="gla",
    )(gla, gla, gla, gla, misc, wa2, ba.reshape(1, QK_W), ng.reshape(1, GV_W),
      jnp.asarray(indv, BF16), jnp.asarray(bd, F32), jnp.asarray(ones, BF16), s0_t)


def _gla_state_to_t(s0):
    bsz = s0.shape[0]
    st = jnp.swapaxes(s0, 2, 3)
    eye = jnp.eye(GLA_HEADS, dtype=s0.dtype)
    full = st[:, :, :, None, :] * eye[None, :, None, :, None]
    return full.reshape(bsz, GV_W, QK_W)


def _gla_state_from_t(st):
    bsz = st.shape[0]
    full = st.reshape(bsz, GLA_HEADS, GLA_DV, GLA_HEADS, GLA_DK)
    diag = jnp.stack([full[:, h, :, h, :] for h in range(GLA_HEADS)], axis=1)
    return jnp.swapaxes(diag, 2, 3)


def _lru_gates(xc, wa_ref, ba_ref, wx_ref, bx_ref, sp_ref):
    r = jax.nn.sigmoid(_dot3(xc, wa_ref[...]) + ba_ref[...])
    i = jax.nn.sigmoid(_dot3(xc, wx_ref[...]) + bx_ref[...])
    log_a = -LRU_C * r * sp_ref[...]
    a = jnp.exp(log_a)
    u = jnp.sqrt(-jnp.tanh(log_a) * (a * a + 1.0)) * (i * xc)
    return a, u


def _lru_prompt_kernel(x_ref, gate_ref, cw_ref, cb_ref, wa_ref, ba_ref, wx_ref, bx_ref, sp_ref,
                       y_ref, hfin_ref, h_scr, tail_scr, a_scr, u_scr, hs_scr, *, T):
    c = pl.program_id(1)

    @pl.when(c == 0)
    def _():
        h_scr[...] = jnp.zeros_like(h_scr)
        tail_scr[...] = jnp.zeros_like(tail_scr)

    x = x_ref[...]
    xx = jnp.concatenate([tail_scr[...], x], axis=0)
    base = 8 - (CONV_W - 1)
    xc = cb_ref[...] + sum(xx[base + j:base + j + T] * cw_ref[j:j + 1, :] for j in range(CONV_W))
    tail_scr[...] = x[T - 8:T]
    a, u = _lru_gates(xc, wa_ref, ba_ref, wx_ref, bx_ref, sp_ref)
    a_scr[...] = a
    u_scr[...] = u

    def group(gi, h):
        off = pl.multiple_of(gi * 8, 8)
        ag = a_scr[pl.ds(off, 8), :]
        ug = u_scr[pl.ds(off, 8), :]
        outs = []
        for j in range(8):
            h = ag[j:j + 1] * h + ug[j:j + 1]
            outs.append(h)
        hs_scr[pl.ds(off, 8), :] = jnp.concatenate(outs, axis=0)
        return h

    h = lax.fori_loop(0, T // 8, group, h_scr[...])
    h_scr[...] = h
    hfin_ref[...] = h
    y_ref[...] = hs_scr[...] * jax.nn.gelu(gate_ref[...])


def _lru_weights(conv_w, conv_b, w_a, b_a, w_x, b_x, lam):
    def bd(w):
        eye = jnp.eye(LRU_HEADS, dtype=w.dtype)
        return (w[:, :, None, :] * eye[:, None, :, None]).reshape(LRU_WIDTH, LRU_WIDTH)
    sp = jax.nn.softplus(-lam.astype(F32)).reshape(1, LRU_WIDTH)
    r1 = lambda t: t.reshape(1, LRU_WIDTH)
    return conv_w, r1(conv_b), bd(w_a), r1(b_a), bd(w_x), r1(b_x), sp


def _lru_prompt_call(lru, weights, T):
    bsz, L, _ = lru.shape
    W = LRU_WIDTH
    const = lambda shape: pl.BlockSpec(shape, lambda b, c: (0,) * len(shape))
    kern = functools.partial(_lru_prompt_kernel, T=T)
    return pl.pallas_call(
        kern,
        grid=(bsz, L // T),
        in_specs=[pl.BlockSpec((None, T, W), lambda b, c: (b, c, 0)),
                  pl.BlockSpec((None, T, W), lambda b, c: (b, c, 1)),
                  const((CONV_W, W)), const((1, W)), const((W, W)), const((1, W)),
                  const((W, W)), const((1, W)), const((1, W))],
        out_specs=[pl.BlockSpec((None, T, W), lambda b, c: (b, c, 0)),
                   pl.BlockSpec((None, 1, W), lambda b, c: (b, 0, 0))],
        out_shape=[jax.ShapeDtypeStruct((bsz, L, W), F32), jax.ShapeDtypeStruct((bsz, 1, W), F32)],
        scratch_shapes=[pltpu.VMEM((1, W), F32), pltpu.VMEM((8, W), F32), pltpu.VMEM((T, W), F32),
                        pltpu.VMEM((T, W), F32), pltpu.VMEM((T, W), F32)],
        compiler_params=_cparams(("parallel", "arbitrary")),
        name="lru_prompt",
    )(lru, lru, *weights)


def _lru_sample_kernel(x_ref, gate_ref, buf_ref, h0_ref, cw_ref, cb_ref, wa_ref, ba_ref, wx_ref, bx_ref, sp_ref,
                       y_ref, hfin_ref, *, L):
    xx = [buf_ref[j] for j in range(CONV_W - 1)] + [x_ref[t] for t in range(L)]
    h = h0_ref[...]
    for t in range(L):
        xc = cb_ref[...] + sum(xx[t + j] * cw_ref[j:j + 1, :] for j in range(CONV_W))
        a, u = _lru_gates(xc, wa_ref, ba_ref, wx_ref, bx_ref, sp_ref)
        h = a * h + u
        y_ref[t] = h * jax.nn.gelu(gate_ref[t])
    hfin_ref[...] = h


def _lru_sample_call(x_t, gate_t, buf_t, h0, weights):
    L, bsz, W = x_t.shape
    kern = functools.partial(_lru_sample_kernel, L=L)
    return pl.pallas_call(
        kern,
        out_shape=[jax.ShapeDtypeStruct((L, bsz, W), F32), jax.ShapeDtypeStruct((bsz, W), F32)],
        compiler_params=pltpu.CompilerParams(vmem_limit_bytes=VMEM_LIMIT),
        name="lru_sample",
    )(x_t, gate_t, buf_t, h0, *weights)


def _later_and_total(tk=ATT_BLK):
    later = np.arange(tk)[:, None] > np.arange(tk)[None, :]
    return jnp.asarray(np.concatenate([later, np.ones((tk, LANES), bool)], axis=1), BF16)


def _sb_logs(z, uo, mask):
    tk = z.shape[1]
    sp = _softplus(z)
    ls = -sp if mask is None else jnp.where(mask, -sp, 0.0)
    hi, lo = _split2(ls)
    res = _dg(hi, uo) + _dg(lo, uo)
    return z - sp, res[:, :tk], res[:, tk:]


def _head_rows(q, n_rows):
    headmask = (_iota((SB_HEADS * n_rows, HW), 0) // n_rows) == (_iota((SB_HEADS * n_rows, HW), 1) // HEAD_DIM)
    qs = jnp.where(headmask, jnp.concatenate([q * (HEAD_DIM ** -0.5)] * SB_HEADS, axis=0), 0.0)
    return qs.astype(BF16), headmask


def _pick_heads(acc, headmask, n_rows):
    return sum(jnp.where(headmask[h * n_rows:(h + 1) * n_rows], acc[h * n_rows:(h + 1) * n_rows], 0.0)
               for h in range(SB_HEADS))


def _sb_prompt_kernel(q_ref, kv_ref, uo_ref, uo2_ref, o_ref, acc_scr, cs_scr, *, BB):
    i = pl.program_id(1)
    Q = ATT_BLK
    R = SB_HEADS * Q
    heads = [_head_rows(q_ref[bi], Q) for bi in range(BB)]
    headmask = heads[0][1]

    def tile(start, tk, uo_ref_, mask):
        uo = uo_ref_[...]
        for bi in range(BB):
            kv = kv_ref[bi, pl.ds(pl.multiple_of(start, Q), tk), :]
            z = _dg(heads[bi][0], kv[:, :HW], NT)
            zl, btw, tot = _sb_logs(z, uo, mask)
            cs = cs_scr[bi]
            e = jnp.exp(zl + (jnp.concatenate([cs] * (tk // LANES), axis=1) + btw))
            w = e if mask is None else jnp.where(mask, e, 0.0)
            cs_scr[bi] = cs + tot
            acc_scr[bi] += _dg(w.astype(BF16), kv[:, HW:])

    acc_scr[...] = jnp.zeros_like(acc_scr)
    cs_scr[...] = jnp.zeros_like(cs_scr)
    tile(i * Q, Q, uo_ref, _iota((R, Q), 1) < _iota((R, Q), 0) % Q)

    @pl.when(i % 2 == 1)
    def _():
        tile((i - 1) * Q, Q, uo_ref, None)

    n_pairs = i // 2

    def alive():
        cs = functools.reduce(jnp.maximum, [cs_scr[bi] for bi in range(BB)])
        return jnp.max(jnp.max(cs, axis=0, keepdims=True)) > -SB_DEAD

    def body(c):
        tile((n_pairs - 1 - c[0]) * 2 * Q, 2 * Q, uo2_ref, None)
        return c[0] + 1, alive()

    lax.while_loop(lambda c: jnp.logical_and(c[0] < n_pairs, c[1]), body, (0, alive()))
    for bi in range(BB):
        o_ref[bi] = _pick_heads(acc_scr[bi], headmask, Q)


def _sb_prompt_call(sbq, sbkv_b):
    bsz, L, _ = sbq.shape
    R = SB_HEADS * ATT_BLK
    BB = _row_tile(bsz, 2)
    return pl.pallas_call(
        functools.partial(_sb_prompt_kernel, BB=BB),
        grid=(bsz // BB, L // ATT_BLK),
        in_specs=[pl.BlockSpec((BB, ATT_BLK, HW), lambda b, i: (b, i, 0)),
                  pl.BlockSpec((BB, L, 2 * HW), lambda b, i: (b, 0, 0)),
                  pl.BlockSpec((ATT_BLK, ATT_BLK + LANES), lambda b, i: (0, 0)),
                  pl.BlockSpec((2 * ATT_BLK, 2 * ATT_BLK + LANES), lambda b, i: (0, 0))],
        out_specs=pl.BlockSpec((BB, ATT_BLK, HW), lambda b, i: (b, i, 0)),
        out_shape=jax.ShapeDtypeStruct((bsz, L, HW), F32),
        scratch_shapes=[pltpu.VMEM((BB, R, HW), F32), pltpu.VMEM((BB, R, ATT_BLK), F32)],
        compiler_params=_cparams(("parallel", "arbitrary")),
        name="sb_prompt",
    )(sbq, sbkv_b, _later_and_total(), _later_and_total(2 * ATT_BLK))


def _sb_sample_kernel(pt_ref, alive_ref, q_ref, init_a_ref, init_b_ref, *rest, L, pps, first):
    page_refs = rest[:pps]
    uo_ref, o_ref, acc_ref, cs_ref = rest[pps:]
    b = pl.program_id(0)
    s = pl.program_id(1)
    R = SB_HEADS * L
    uo = uo_ref[...]
    q16, headmask = _head_rows(q_ref[...], L)

    @pl.when(s == 0)
    def _():
        if first:
            new = init_a_ref[...]
            mask = _iota((R, ATT_BLK), 1) < (_iota((R, ATT_BLK), 0) % L)
            zl, btw, tot = _sb_logs(_dg(q16, new[:, :HW].astype(BF16), NT), uo, mask)
            w = jnp.where(mask, jnp.exp(zl + btw), 0.0)
            cs_ref[...] = tot
            acc_ref[...] = _dg(w.astype(BF16), new[:, HW:].astype(BF16))
        else:
            acc_ref[...] = init_a_ref[...]
            cs_ref[...] = init_b_ref[...]

    @pl.when(alive_ref[b] > 0)
    def _():
        z = jnp.concatenate([_dg(q16, page_refs[r][:HW, :].astype(BF16)) for r in range(pps)], axis=0)
        zl, btw, tot = _sb_logs(z, uo, None)
        cs = cs_ref[...]
        acc = acc_ref[...]
        for r in range(pps):
            sl = slice(r * R, (r + 1) * R)
            w = jnp.exp(zl[sl] + (cs + btw[sl]))
            acc = acc + _dg(w.astype(BF16), page_refs[r][HW:, :].astype(BF16), NT)
            cs = cs + tot[sl]
        acc_ref[...] = acc
        cs_ref[...] = cs

    @pl.when(s == pl.num_programs(1) - 1)
    def _():
        o_ref[...] = _pick_heads(acc_ref[...], headmask, L)


def _sb_sample_stage(sbq, init_a, init_b, alive, cache_t, page_table, layer, pps, first):
    bsz, L, _ = sbq.shape
    n_pages = page_table.shape[1]
    steps = 1 if first else n_pages // pps - 1
    skip = 0 if first else 1
    R = SB_HEADS * L
    kern = functools.partial(_sb_sample_kernel, L=L, pps=pps, first=first)

    def page_spec(r):
        return pl.BlockSpec((None, None, 2 * HW, ATT_BLK),
                            lambda b, s, pt, al: (layer, jnp.where(al[b] > 0,
                                                                   pt[b, n_pages - 1 - ((s + skip) * pps + r)], 0),
                                                  0, 0))

    per_b = lambda shape: pl.BlockSpec((None,) + shape, lambda b, s, pt, al: (b, 0, 0))
    grid_spec = pltpu.PrefetchScalarGridSpec(
        num_scalar_prefetch=2,
        grid=(bsz, steps),
        in_specs=[per_b((L, HW)), per_b(init_a.shape[1:]), per_b(init_b.shape[1:])]
                 + [page_spec(r) for r in range(pps)]
                 + [pl.BlockSpec((ATT_BLK, 2 * ATT_BLK), lambda b, s, pt, al: (0, 0))],
        out_specs=[per_b((L, HW)), per_b((R, HW)), per_b((R, ATT_BLK))],
    )
    return pl.pallas_call(
        kern,
        grid_spec=grid_spec,
        out_shape=[jax.ShapeDtypeStruct((bsz, L, HW), F32), jax.ShapeDtypeStruct((bsz, R, HW), F32),
                   jax.ShapeDtypeStruct((bsz, R, ATT_BLK), F32)],
        compiler_params=_cparams(("parallel", "arbitrary")),
        name="sb_sample",
    )(page_table, alive, sbq, init_a, init_b, *([cache_t] * pps), _later_and_total())


def _sb_sample_call(sbq, new_kv, cache_t, page_table, layer, pps):
    bsz, L, _ = sbq.shape
    R = SB_HEADS * L
    y, acc, cs = _sb_sample_stage(sbq, new_kv, jnp.zeros((bsz, R, ATT_BLK), F32), jnp.ones((bsz,), jnp.int32),
                                  cache_t, page_table, layer, pps, True)
    if page_table.shape[1] // pps > 1:
        alive = (jnp.max(cs, axis=(1, 2)) > -SB_DEAD).astype(jnp.int32)
        y = lax.cond(jnp.any(alive > 0),
                     lambda: _sb_sample_stage(sbq, acc, cs, alive, cache_t, page_table, layer, pps, False)[0],
                     lambda: y)
    return y


def _masked_softmax(s, valid):
    sm = jnp.where(valid, s, NEG)
    e = jnp.where(valid, jnp.exp(sm - jnp.max(sm, axis=1, keepdims=True)), 0.0)
    return e / jnp.maximum(jnp.sum(e, axis=1, keepdims=True), 1e-30)


def _top_select(score, n_sel):
    T, N = score.shape
    midx = _iota((N, N), 0)
    nidx = _iota((N, N), 1)
    ranks = []
    for t in range(T):
        row = score[t:t + 1, :]
        col = jnp.sum(jnp.where(midx == nidx, row, 0.0), axis=1, keepdims=True)
        beats = jnp.where(col > row, 1.0, jnp.where((col == row) & (midx < nidx), 1.0, 0.0))
        ranks.append(jnp.sum(beats, axis=0, keepdims=True))
    return jnp.where(jnp.concatenate(ranks, axis=0) < n_sel, 1.0, 0.0)


def _nsa_prompt_kernel(q_ref, rows_ref, win_ref, misc_ref, ex_ref, o_ref,
                       kcvc_scr, kse_scr, vse_scr, kwe_scr, vwe_scr, score_scr, pick_scr, s_scr, mrun_scr,
                       m_scr, acc_scr, *, L):
    i = pl.program_id(1)
    nb = L // NSA_BLOCK
    Q = NSA_Q
    G = NSA_HEADS
    R = G * Q

    @pl.when(i == 0)
    def _():
        blk = rows_ref[:, :2 * HEAD_DIM].reshape(nb, NSA_BLOCK, 2 * HEAD_DIM)
        kcvc_scr[...] = jnp.sum(blk, axis=1) * (1.0 / NSA_BLOCK)
        ksvs = rows_ref[:, 2 * HEAD_DIM:]
        kse_scr[...] = _low_half(ksvs).astype(BF16)
        vse_scr[...] = _low_half(_swap_halves(ksvs), 1.0).astype(BF16)
        kwvw = win_ref[...]
        kwe_scr[...] = _low_half(kwvw).astype(BF16)
        vwe_scr[...] = _low_half(_swap_halves(kwvw), 1.0).astype(BF16)

    q4 = (_heads_to_rows(q_ref[...]) * (HEAD_DIM ** -0.5)).astype(BF16)
    qpos1 = i * Q + _iota((Q, 1), 0)
    qpos4 = jnp.concatenate([qpos1] * G, axis=0)

    kcvc = kcvc_scr[...]
    s_c = _dg(q4, _low_half(kcvc).astype(BF16), NT)
    valid_c = (_iota((R, nb), 1) + 1) * NSA_BLOCK - 1 <= qpos4
    p_c = _masked_softmax(s_c, valid_c)
    o_c = _dg(p_c.astype(BF16), _swap_halves(kcvc).astype(BF16))

    imp = sum(p_c[h * Q:(h + 1) * Q] for h in range(G))
    blk1 = _iota((Q, nb), 1)
    score = jnp.where(blk1 == qpos1 // NSA_BLOCK, NSA_FORCE, jnp.where(blk1 * NSA_BLOCK <= qpos1, imp, -1.0))
    score_t = score.T
    score_scr[...] = score_t
    nidx = _iota((nb, Q), 0)

    def rank_body(m, rank):
        row = score_scr[pl.ds(m, 1), :]
        earlier = jnp.where(nidx > m, 1.0, 0.0)
        return rank + jnp.where(row > score_t, 1.0, jnp.where(row == score_t, earlier, 0.0))

    rank = lax.fori_loop(0, nb, rank_body, jnp.zeros((nb, Q), F32), unroll=8)
    sel_t = jnp.where(rank < min(NSA_TOPK, nb), 1.0, 0.0).astype(BF16)
    picked = _dg(sel_t, ex_ref[...], TN)
    causal = _iota((Q, L), 1) <= qpos1
    pick_scr[...] = jnp.where(causal & (picked > 0.5), 0.0, NEG)

    n_tiles = (i * Q + Q + SEL_TILE - 1) // SEL_TILE
    n_chunks = SEL_TILE // LANES
    mrun_scr[...] = jnp.full(mrun_scr.shape, NEG, F32)

    def pass1(j, c):
        off = pl.multiple_of(j * SEL_TILE, SEL_TILE)
        bias = pick_scr[:, pl.ds(off, SEL_TILE)]
        kb = kse_scr[pl.ds(off, SEL_TILE), :]
        for h in range(G):
            sl = slice(h * Q, (h + 1) * Q)
            sm = _dg(q4[sl], kb, NT) + bias
            s_scr[sl, pl.ds(off, SEL_TILE)] = sm
            mh = functools.reduce(jnp.maximum, [sm[:, k * LANES:(k + 1) * LANES] for k in range(n_chunks)])
            mrun_scr[sl, :] = jnp.maximum(mrun_scr[sl, :], mh)
        return c

    lax.fori_loop(0, n_tiles, pass1, 0)
    m_scr[...] = jnp.broadcast_to(jnp.max(mrun_scr[...], axis=1, keepdims=True), (R, LANES))
    acc_scr[...] = jnp.zeros_like(acc_scr)

    def pass2(j, c):
        off = pl.multiple_of(j * SEL_TILE, SEL_TILE)
        vb = vse_scr[pl.ds(off, SEL_TILE), :]
        for h in range(G):
            sl = slice(h * Q, (h + 1) * Q)
            p = jnp.exp(s_scr[sl, pl.ds(off, SEL_TILE)] - jnp.concatenate([m_scr[sl, :]] * n_chunks, axis=1))
            acc_scr[sl, :] += _dg(p.astype(BF16), vb)
        return c

    lax.fori_loop(0, n_tiles, pass2, 0)
    acc = acc_scr[...]
    o_s = acc / jnp.maximum(_swap_halves(acc), 1e-30)

    col = _iota((Q, ATT_BLK), 1)
    n_win = (NSA_WINDOW + Q) // ATT_BLK
    last_tile = (i + 1) * (Q // ATT_BLK) - 1
    offs, biases = [], []
    for t in range(n_win):
        j = last_tile - t
        offs.append(pl.multiple_of(jnp.maximum(j, 0) * ATT_BLK, ATT_BLK))
        kpos = j * ATT_BLK + col
        dist = qpos1 - kpos
        biases.append(jnp.where((dist >= 0) & (dist <= NSA_WINDOW) & (kpos >= 0), 0.0, NEG))
    bias_w = jnp.concatenate(biases, axis=1)
    kw = jnp.concatenate([kwe_scr[pl.ds(o, ATT_BLK), :] for o in offs], axis=0)
    vw = jnp.concatenate([vwe_scr[pl.ds(o, ATT_BLK), :] for o in offs], axis=0)
    o_w_parts = []
    for h in range(G):
        sm = _dg(q4[h * Q:(h + 1) * Q], kw, NT) + bias_w
        p = jnp.exp(sm - jnp.max(sm, axis=1, keepdims=True))
        o_w_parts.append(_dg(p.astype(BF16), vw))
    acc = jnp.concatenate(o_w_parts, axis=0)
    o_w = acc / jnp.maximum(_swap_halves(acc), 1e-30)

    gates = jax.nn.sigmoid(misc_ref[:, GLA_RANK:GLA_RANK + 3 * G])
    parts = []
    for h in range(G):
        sl = slice(h * Q, (h + 1) * Q)
        parts.append(gates[:, 3 * h:3 * h + 1] * o_c[sl] + gates[:, 3 * h + 1:3 * h + 2] * o_s[sl]
                     + gates[:, 3 * h + 2:3 * h + 3] * o_w[sl])
    o_ref[...] = _rows_to_heads(parts)


def _nsa_prompt_call(nq, rows, win, misc):
    bsz, L, _ = nq.shape
    nb = L // NSA_BLOCK
    Q = NSA_Q
    R = NSA_HEADS * Q
    expand = jnp.asarray(np.arange(nb)[:, None] == np.arange(L)[None, :] // NSA_BLOCK, BF16)
    kern = functools.partial(_nsa_prompt_kernel, L=L)
    return pl.pallas_call(
        kern,
        grid=(bsz, L // Q),
        in_specs=[pl.BlockSpec((None, Q, HW), lambda b, i: (b, i, 0)),
                  pl.BlockSpec((None, L, HW), lambda b, i: (b, 0, 0)),
                  pl.BlockSpec((None, L, 2 * HEAD_DIM), lambda b, i: (b, 0, 0)),
                  pl.BlockSpec((None, Q, LANES), lambda b, i: (b, i, 0)),
                  pl.BlockSpec((nb, L), lambda b, i: (0, 0))],
        out_specs=pl.BlockSpec((None, Q, HW), lambda b, i: (b, i, 0)),
        out_shape=jax.ShapeDtypeStruct((bsz, L, HW), F32),
        scratch_shapes=[pltpu.VMEM((nb, LANES), F32),
                        pltpu.VMEM((L, LANES), BF16), pltpu.VMEM((L, LANES), BF16),
                        pltpu.VMEM((L, LANES), BF16), pltpu.VMEM((L, LANES), BF16),
                        pltpu.VMEM((nb, Q), F32), pltpu.VMEM((Q, L), F32),
                        pltpu.VMEM((R, L), F32), pltpu.VMEM((R, LANES), F32),
                        pltpu.VMEM((R, LANES), F32), pltpu.VMEM((R, LANES), F32)],
        compiler_params=_cparams(("parallel", "arbitrary")),
        name="nsa_prompt",
    )(nq, rows, win, misc, expand)


def _nsa_sample_kernel(pt_ref, q_ref, new_ref, winp_ref, winn_ref, misc_ref, ea_ref, eb_ref, *rest,
                       L, pps, n_pages):
    page_refs = rest[:pps]
    o_ref, kcvc_scr, ksvs_scr = rest[pps:]
    s = pl.program_id(1)
    steps = n_pages // pps
    G = NSA_HEADS
    R = G * L
    per_page = ATT_BLK // NSA_BLOCK
    bps = pps * per_page
    gk = pps * ATT_BLK
    nbp = n_pages * per_page
    nbp_pad = kcvc_scr.shape[1]
    q4 = (_heads_to_rows(q_ref[...]) * (HEAD_DIM ** -0.5)).astype(BF16)
    t_row = _iota((R, 1), 0) % L

    @pl.when(s == 0)
    def _():
        kcvc_scr[...] = jnp.zeros_like(kcvc_scr)

    x = jnp.concatenate([page_refs[r][:2 * HEAD_DIM, :] for r in range(pps)], axis=1)
    means = _dot_xl(x, ea_ref[...])
    base = (s * bps) % LANES
    off = pl.multiple_of(((s * bps) // LANES) * LANES, LANES)
    kcvc_scr[:, pl.ds(off, LANES)] += pltpu.roll(means, base, 1)
    for r in range(pps):
        koff = pl.multiple_of((s * pps + r) * ATT_BLK, ATT_BLK)
        ksvs_scr[:, pl.ds(koff, ATT_BLK)] = page_refs[r][2 * HEAD_DIM:, :].astype(BF16)

    @pl.when(s == steps - 1)
    def _():
        kcvc = kcvc_scr[...].astype(BF16)
        live = _iota((R, nbp_pad), 1) < nbp
        s_c = _dg(q4, kcvc)
        p_c = _masked_softmax(s_c, live)
        o_c = _swap_halves(_dg(p_c.astype(BF16), kcvc, NT))
        imp = sum(p_c[h * L:(h + 1) * L] for h in range(G))
        imp = jnp.where(live[:L], imp, -3.0)
        sel = _top_select(imp, min(NSA_TOPK, nbp + 1) - 1)
        sel = jnp.concatenate([sel] * G, axis=0).astype(BF16)
        n_buf = winp_ref.shape[1]
        winp = winp_ref[...].astype(BF16)
        winn = winn_ref[...]
        dist_p = t_row + n_buf - _iota((R, n_buf), 1)
        dist_n = t_row - _iota((R, ATT_BLK), 1)
        s_w = jnp.concatenate([_dg(q4, winp), _dg(q4, _low_half(winn).astype(BF16), NT)], axis=1)
        valid_w = jnp.concatenate([(dist_p >= 0) & (dist_p <= NSA_WINDOW),
                                   (dist_n >= 0) & (dist_n <= NSA_WINDOW)], axis=1)
        p_w = _masked_softmax(s_w, valid_w).astype(BF16)
        o_w = _swap_halves(_dg(p_w[:, :n_buf], winp, NT) + _dg(p_w[:, n_buf:], winn.astype(BF16)))
        ksvs_new = new_ref[:, 2 * HEAD_DIM:]
        sms = [jnp.where(dist_n >= 0, _dg(q4, _low_half(ksvs_new).astype(BF16), NT), NEG)]
        for g in range(steps):
            picked = _dg(sel[:, g * bps:(g + 1) * bps], eb_ref[...]) > 0.5
            sms.append(jnp.where(picked, _dg(q4, ksvs_scr[:, g * gk:(g + 1) * gk]), NEG))
        m = functools.reduce(jnp.maximum, [jnp.max(t, axis=1, keepdims=True) for t in sms])
        p0 = jnp.exp(sms[0] - m)
        l = jnp.sum(p0, axis=1, keepdims=True)
        acc = _dg(p0.astype(BF16), ksvs_new.astype(BF16))
        for g in range(steps):
            p = jnp.exp(sms[g + 1] - m)
            l = l + jnp.sum(p, axis=1, keepdims=True)
            acc = acc + _dg(p.astype(BF16), ksvs_scr[:, g * gk:(g + 1) * gk], NT)
        o_s = _swap_halves(acc) / jnp.maximum(l, 1e-30)
        gates = jax.nn.sigmoid(misc_ref[:, GLA_RANK:GLA_RANK + 3 * G])
        parts = []
        for h in range(G):
            sl = slice(h * L, (h + 1) * L)
            parts.append(gates[:, 3 * h:3 * h + 1] * o_c[sl] + gates[:, 3 * h + 1:3 * h + 2] * o_s[sl]
                         + gates[:, 3 * h + 2:3 * h + 3] * o_w[sl])
        o_ref[...] = _rows_to_heads(parts)


def _nsa_sample_call(nq, new_rows, win_past_t, win_new, misc, cache_t, page_table, layer, pps):
    bsz, L, _ = nq.shape
    n_pages = page_table.shape[1]
    steps = n_pages // pps
    n_buf = win_past_t.shape[3]
    per_page = ATT_BLK // NSA_BLOCK
    bps = pps * per_page
    nbp = n_pages * per_page
    nbp_pad = -(-nbp // LANES) * LANES
    assert LANES % bps == 0
    R = NSA_HEADS * L
    keys = np.arange(pps * ATT_BLK)
    ea = jnp.asarray((keys[:, None] // NSA_BLOCK == np.arange(LANES)[None, :]) / float(NSA_BLOCK), BF16)
    eb = jnp.asarray(np.arange(bps)[:, None] == keys[None, :] // NSA_BLOCK, BF16)
    kern = functools.partial(_nsa_sample_kernel, L=L, pps=pps, n_pages=n_pages)

    def page_spec(r):
        return pl.BlockSpec((None, None, HW, ATT_BLK), lambda b, s, pt: (layer, pt[b, s * pps + r], 0, 0))

    cst = lambda shape: pl.BlockSpec(shape, lambda b, s, pt: (0,) * len(shape))
    grid_spec = pltpu.PrefetchScalarGridSpec(
        num_scalar_prefetch=1,
        grid=(bsz, steps),
        in_specs=[pl.BlockSpec((None, L, HW), lambda b, s, pt: (b, 0, 0)),
                  pl.BlockSpec((None, ATT_BLK, HW), lambda b, s, pt: (b, 0, 0)),
                  pl.BlockSpec((None, None, 2 * HEAD_DIM, n_buf), lambda b, s, pt: (layer, b, 0, 0)),
                  pl.BlockSpec((None, ATT_BLK, 2 * HEAD_DIM), lambda b, s, pt: (b, 0, 0)),
                  pl.BlockSpec((None, L, LANES), lambda b, s, pt: (b, 0, 0)),
                  cst((pps * ATT_BLK, LANES)), cst((bps, pps * ATT_BLK))]
                 + [page_spec(r) for r in range(pps)],
        out_specs=pl.BlockSpec((None, L, HW), lambda b, s, pt: (b, 0, 0)),
        scratch_shapes=[pltpu.VMEM((2 * HEAD_DIM, nbp_pad), F32),
                        pltpu.VMEM((2 * HEAD_DIM, n_pages * ATT_BLK), BF16)],
    )
    return pl.pallas_call(
        kern,
        grid_spec=grid_spec,
        out_shape=jax.ShapeDtypeStruct((bsz, L, HW), F32),
        compiler_params=_cparams(("parallel", "arbitrary")),
        name="nsa_sample",
    )(page_table, nq, new_rows, win_past_t, win_new, misc, ea, eb, *([cache_t] * pps))


def _outproj_kernel(ya_ref, yb_ref, yc_ref, yd_ref, x_ref, w_ref, g_ref, b_ref, o_ref):
    y = jnp.concatenate([ya_ref[...], yb_ref[...], yc_ref[...], yd_ref[...]], axis=1).astype(BF16)
    h = jnp.dot(y, w_ref[...], preferred_element_type=F32)
    o_ref[...] = _layer_norm(DN_ALPHA * x_ref[...] + h, g_ref[...], b_ref[...])


def _outproj_call(ya, yb, yc, yd, x, w, g, b, tm):
    n = x.shape[0]
    part = lambda: pl.BlockSpec((tm, HW), lambda i: (i, 0))
    return pl.pallas_call(
        _outproj_kernel,
        grid=(n // tm,),
        in_specs=[part(), part(), part(), part(),
                  pl.BlockSpec((tm, D_MODEL), lambda i: (i, 0)),
                  pl.BlockSpec((4 * HW, D_MODEL), lambda i: (0, 0)),
                  pl.BlockSpec((1, D_MODEL), lambda i: (0, 0)),
                  pl.BlockSpec((1, D_MODEL), lambda i: (0, 0))],
        out_specs=pl.BlockSpec((tm, D_MODEL), lambda i: (i, 0)),
        out_shape=jax.ShapeDtypeStruct((n, D_MODEL), F32),
        compiler_params=_cparams(("parallel",)),
        name="out_proj_ln",
    )(ya, yb, yc, yd, x, w, g.reshape(1, D_MODEL), b.reshape(1, D_MODEL))


def _route(logits):
    lane = _iota(logits.shape, 1)
    live = lane < N_EXPERTS
    lg = jnp.where(live, logits, NEG)
    e = jnp.where(live, jnp.exp(lg - jnp.max(lg, axis=1, keepdims=True)), 0.0)
    probs = e / jnp.sum(e, axis=1, keepdims=True)
    big = logits.shape[1]

    def top2(vals):
        w1 = jnp.max(vals, axis=1, keepdims=True)
        i1 = jnp.min(jnp.where(vals == w1, lane, big), axis=1, keepdims=True)
        rest = jnp.where(lane == i1, -2.0, vals)
        w2 = jnp.max(rest, axis=1, keepdims=True)
        i2 = jnp.min(jnp.where(rest == w2, lane, big), axis=1, keepdims=True)
        return w1, i1, w2, i2

    best = None
    g_sel = None
    for g in range(N_GROUPS):
        in_g = (lane // EXPERTS_PER_GROUP) == g
        w1, _, w2, _ = top2(jnp.where(in_g, probs, -1.0))
        tot = w1 + w2
        if best is None:
            best, g_sel = tot, jnp.zeros_like(tot, dtype=jnp.int32)
        else:
            upd = tot > best
            g_sel = jnp.where(upd, g, g_sel)
            best = jnp.where(upd, tot, best)
    in_grp = live & ((lane // EXPERTS_PER_GROUP) == g_sel)
    w1, i1, w2, i2 = top2(jnp.where(in_grp, probs, -1.0))
    den = w1 + w2
    return jnp.where(lane == i1, w1 / den, 0.0) + jnp.where(lane == i2, w2 / den, 0.0)


def _moe_kernel(x_ref, rw_ref, wg_ref, wu_ref, wd_ref, g_ref, b_ref, o_ref, acc_scr, comb_scr, xb_scr):
    e = pl.program_id(1)

    @pl.when(e == 0)
    def _():
        x = x_ref[...]
        xb_scr[...] = x.astype(BF16)
        comb_scr[...] = _route(_dot3(x, rw_ref[...]))
        acc_scr[...] = jnp.zeros_like(acc_scr)

    xb = xb_scr[...]
    hg = jnp.dot(xb, wg_ref[...], preferred_element_type=F32)
    hu = jnp.dot(xb, wu_ref[...], preferred_element_type=F32)
    comb = comb_scr[...]
    c = jnp.sum(jnp.where(_iota(comb.shape, 1) == e, comb, 0.0), axis=1, keepdims=True)
    h = (hg * jax.nn.sigmoid(hg)) * hu * c
    acc_scr[...] += jnp.dot(h.astype(BF16), wd_ref[...], preferred_element_type=F32)

    @pl.when(e == pl.num_programs(1) - 1)
    def _():
        o_ref[...] = _layer_norm(DN_ALPHA * x_ref[...] + acc_scr[...], g_ref[...], b_ref[...])


def _moe_call(x, rw, wg, wu, wd, g, b, tm):
    n = x.shape[0]
    return pl.pallas_call(
        _moe_kernel,
        grid=(n // tm, N_EXPERTS),
        in_specs=[pl.BlockSpec((tm, D_MODEL), lambda i, e: (i, 0)),
                  pl.BlockSpec((D_MODEL, LANES), lambda i, e: (0, 0)),
                  pl.BlockSpec((None, D_MODEL, D_EXPERT), lambda i, e: (e, 0, 0)),
                  pl.BlockSpec((None, D_MODEL, D_EXPERT), lambda i, e: (e, 0, 0)),
                  pl.BlockSpec((None, D_EXPERT, D_MODEL), lambda i, e: (e, 0, 0)),
                  pl.BlockSpec((1, D_MODEL), lambda i, e: (0, 0)),
                  pl.BlockSpec((1, D_MODEL), lambda i, e: (0, 0))],
        out_specs=pl.BlockSpec((tm, D_MODEL), lambda i, e: (i, 0)),
        out_shape=jax.ShapeDtypeStruct((n, D_MODEL), F32),
        scratch_shapes=[pltpu.VMEM((tm, D_MODEL), F32), pltpu.VMEM((tm, LANES), F32),
                        pltpu.VMEM((tm, D_MODEL), BF16)],
        compiler_params=_cparams(("parallel", "arbitrary")),
        name="moe_ln",
    )(x, rw, wg, wu, wd, g.reshape(1, D_MODEL), b.reshape(1, D_MODEL))


def _pad_rows(t, n):
    return jnp.pad(t, ((0, 0), (0, n - t.shape[1]), (0, 0)))


def _row_tile(n, pref):
    t = min(pref, n)
    while n % t:
        t //= 2
    return t


def _keys_minor(cache):
    d, p, rows = cache.shape[:3]
    return jnp.transpose(cache, (0, 1, 3, 4, 5, 2)).reshape(d, p, -1, rows)


def kernel(x_prompt, x_sample, cache_sb_kv, cache_nsa_kv, cache_nsa_win, state_gla, state_lru_h, state_lru_conv, page_table, ln_in_g, ln_in_b, w_in, gla_w_a2, gla_b_a, gla_norm_g, lru_conv_w, lru_conv_b, lru_w_a, lru_b_a, lru_w_x, lru_b_x, lru_lambda, w_out, ln1_g, ln1_b, router_w, moe_w_gate, moe_w_up, moe_w_down, ln2_g, ln2_b):
    bp, lp, d = x_prompt.shape
    bs, ls, _ = x_sample.shape
    depth = w_in.shape[0]
    page = cache_sb_kv.shape[2]
    n_pages = page_table.shape[1]
    past_len = n_pages * page
    n_buf = cache_nsa_win.shape[2]
    n_p, n_s = bp * lp, bs * ls
    assert page == ATT_BLK and lp % SEL_TILE == 0 and ls <= 8 and d == D_MODEL
    assert n_pages * (ATT_BLK // NSA_BLOCK) >= NSA_TOPK - 1 and n_buf <= past_len

    tm_p = _row_tile(n_p, 512)
    tm_s = _row_tile(n_s, 512)
    tab_p = _rope_table(jnp.arange(lp))
    tab_s = jnp.tile(_rope_table(past_len + jnp.arange(ls)), (bs, 1))
    assert lp % tm_p == 0 and tm_s == n_s

    cache_sb_t = _keys_minor(cache_sb_kv)
    cache_nsa_t = _keys_minor(cache_nsa_kv)
    win_past_t = _keys_minor(cache_nsa_win)
    win_past = cache_nsa_win.reshape(depth, bs, n_buf, 2 * HEAD_DIM)
    rw = jnp.pad(router_w, ((0, 0), (0, LANES - N_EXPERTS)))
    pps = 16
    while n_pages % pps:
        pps //= 2
    gla_c = 32
    lru_t = _row_tile(lp, 512)

    xp = _ln_call(x_prompt.reshape(n_p, d), ln_in_g, ln_in_b, tm_p)
    xs = _ln_call(x_sample.reshape(n_s, d), ln_in_g, ln_in_b, tm_s)

    st_p, st_s = [], []
    for l in range(depth):
        w_l = _prep_w_in(w_in[l])
        lru_w = _lru_weights(lru_conv_w[l], lru_conv_b[l], lru_w_a[l], lru_b_a[l], lru_w_x[l], lru_b_x[l],
                             lru_lambda[l])
        w_out_l = w_out[l].astype(BF16)
        wg, wu, wd = moe_w_gate[l].astype(BF16), moe_w_up[l].astype(BF16), moe_w_down[l].astype(BF16)

        gla, lru, sbq, sbkv, nq, nsa, win, misc, sbkv_b = _proj_call(xp, w_l, tab_p, tm_p)
        r3 = lambda t, b_, l_: t.reshape(b_, l_, t.shape[-1])
        gla3, lru3, sbq3, sbkv3 = r3(gla, bp, lp), r3(lru, bp, lp), r3(sbq, bp, lp), r3(sbkv, bp, lp)
        nq3, nsa3, win3, misc3 = r3(nq, bp, lp), r3(nsa, bp, lp), r3(win, bp, lp), r3(misc, bp, lp)
        y_a, gla_t = _gla_call(gla3, misc3, gla_w_a2[l], gla_b_a[l], gla_norm_g[l],
                               jnp.zeros((bp, GV_W, QK_W), F32), gla_c, gla_c)
        y_b, lru_h = _lru_prompt_call(lru3, lru_w, lru_t)
        y_c = _sb_prompt_call(sbq3, sbkv_b.reshape(bp, lp, 2 * HW))
        y_d = _nsa_prompt_call(nq3, nsa3, win3, misc3)
        x1 = _outproj_call(y_a.reshape(n_p, HW), y_b.reshape(n_p, HW), y_c.reshape(n_p, HW), y_d.reshape(n_p, HW),
                           xp, w_out_l, ln1_g[l], ln1_b[l], tm_p)
        xp = _moe_call(x1, rw, wg, wu, wd, ln2_g[l], ln2_b[l], _row_tile(n_p, 1024))
        wn = min(NSA_WINDOW, lp)
        st_p.append((sbkv3.reshape(bp, lp, 2, SB_HEADS, HEAD_DIM), nsa3.reshape(bp, lp, 4, 1, HEAD_DIM),
                     win3[:, lp - wn:].reshape(bp, wn, 2, 1, HEAD_DIM), _gla_state_from_t(gla_t),
                     lru_h.reshape(bp, LRU_WIDTH), lru3[:, lp - (CONV_W - 1):, :LRU_WIDTH]))

        gla, lru, sbq, sbkv, nq, nsa, win, misc, _ = _proj_call(xs, w_l, tab_s, tm_s)
        gla3, lru3, sbq3, sbkv3 = r3(gla, bs, ls), r3(lru, bs, ls), r3(sbq, bs, ls), r3(sbkv, bs, ls)
        nq3, nsa3, win3, misc3 = r3(nq, bs, ls), r3(nsa, bs, ls), r3(win, bs, ls), r3(misc, bs, ls)
        gc = 16
        y_a, gla_t = _gla_call(_pad_rows(gla3, gc), _pad_rows(misc3, gc), gla_w_a2[l], gla_b_a[l], gla_norm_g[l],
                               _gla_state_to_t(state_gla[l].astype(F32)), gc, ls)
        y_a = y_a[:, :ls]
        tmaj = lambda t: jnp.swapaxes(t, 0, 1)
        y_b, lru_h = _lru_sample_call(tmaj(lru3[:, :, :LRU_WIDTH]), tmaj(lru3[:, :, LRU_WIDTH:]),
                                      tmaj(state_lru_conv[l]), state_lru_h[l].astype(F32), lru_w)
        y_b = tmaj(y_b)
        y_c = _sb_sample_call(sbq3, _pad_rows(sbkv3, ATT_BLK), cache_sb_t, page_table, l, pps)
        y_d = _nsa_sample_call(nq3, _pad_rows(nsa3, ATT_BLK), win_past_t, _pad_rows(win3, ATT_BLK), misc3,
                               cache_nsa_t, page_table, l, 2 * pps if n_pages % (2 * pps) == 0 else pps)
        x1 = _outproj_call(y_a.reshape(n_s, HW), y_b.reshape(n_s, HW), y_c.reshape(n_s, HW), y_d.reshape(n_s, HW),
                           xs, w_out_l, ln1_g[l], ln1_b[l], tm_s)
        xs = _moe_call(x1, rw, wg, wu, wd, ln2_g[l], ln2_b[l], tm_s)
        win_all = jnp.concatenate([win_past[l], win3], axis=1)
        lru_buf = jnp.concatenate([state_lru_conv[l], lru3[:, :, :LRU_WIDTH]], axis=1)[:, ls:]
        st_s.append((sbkv3.reshape(bs, ls, 2, SB_HEADS, HEAD_DIM), nsa3.reshape(bs, ls, 4, 1, HEAD_DIM),
                     win_all[:, win_all.shape[1] - n_buf:].reshape(bs, n_buf, 2, 1, HEAD_DIM),
                     _gla_state_from_t(gla_t), lru_h, lru_buf))

    outs_p = [jnp.stack([st[i] for st in st_p]) for i in range(6)]
    outs_s = [jnp.stack([st[i] for st in st_s]) for i in range(6)]
    res = [xp.reshape(bp, lp, d), xs.reshape(bs, ls, d)]
    for a, b in zip(outs_p, outs_s):
        res += [a, b]
    return tuple(res)
```
